```python
import jax, jax.numpy as jnp
from jax import lax
import numpy as np

D_MODEL = 1024
BATCH = 4
SEQ = 8192
DEPTH = 2
DEC_BATCH = 8
DEC_SEQ = 32
PAST_LEN = 2048

CHUNK = 64
Q_BLOCK = 128
PLE_DIM = 256
D_FF = 4 * D_MODEL
MLA_HEADS = 8
MLA_NOPE = 64
MLA_ROPE = 32
MLA_V = 64
MLA_Q_LORA = 384
MLA_KV_LORA = 256
MLA_SCALE = (MLA_NOPE + MLA_ROPE) ** -0.5
RET_HEADS = 8
RET_DK = 32
RET_DV = 64
D_MIX = MLA_HEADS * MLA_V + RET_HEADS * RET_DV
ROPE_THETA = 10000.0
EPS = 1e-6
NEG = -1e30
IN_SPLITS = (MLA_Q_LORA, MLA_KV_LORA, MLA_ROPE, RET_HEADS * RET_DK, RET_HEADS * RET_DK,
             RET_HEADS * RET_DV, RET_HEADS * RET_DV)
D_IN = sum(IN_SPLITS)

kernel_name = "hybrid_mla_retention_streaming_step"


def rms_norm(x, w):
    xf = x.astype(jnp.float32)
    y = xf * lax.rsqrt(jnp.mean(xf * xf, axis=-1, keepdims=True) + EPS)
    return (y * w.astype(jnp.float32)).astype(x.dtype)


def rope(x, pos):
    half = x.shape[-1] // 2
    inv = ROPE_THETA ** (-jnp.arange(half, dtype=jnp.float32) / half)
    ang = pos.astype(jnp.float32)[:, None] * inv[None, :]
    if x.ndim == 4:
        ang = ang[:, None, :]
    cos, sin = jnp.cos(ang), jnp.sin(ang)
    xf = x.astype(jnp.float32)
    x1, x2 = xf[..., :half], xf[..., half:]
    return jnp.concatenate([x1 * cos - x2 * sin, x1 * sin + x2 * cos], axis=-1).astype(x.dtype)


def project_mixers(a, pos, w_in, q_norm_w, w_uq, kv_norm_w):
    B, L, _ = a.shape
    offs = np.cumsum(IN_SPLITS)[:-1].tolist()
    q_lat, ckv, krope, rq, rk, rv, rg = jnp.split(a @ w_in, offs, axis=-1)
    q = (rms_norm(q_lat, q_norm_w) @ w_uq).reshape(B, L, MLA_HEADS, MLA_NOPE + MLA_ROPE)
    q_nope = q[..., :MLA_NOPE]
    q_rope = rope(q[..., MLA_NOPE:], pos)
    ckv = rms_norm(ckv, kv_norm_w)
    krope = rope(krope, pos)
    rq = rope(rq.reshape(B, L, RET_HEADS, RET_DK), pos)
    rk = rope(rk.reshape(B, L, RET_HEADS, RET_DK), pos) * (RET_DK ** -0.5)
    rv = rv.reshape(B, L, RET_HEADS, RET_DV)
    return q_nope, q_rope, ckv, krope, rq, rk, rv, rg


def expand_latent(ckv, w_ukv):
    B, L, _ = ckv.shape
    kv = (ckv @ w_ukv).reshape(B, L, MLA_HEADS, MLA_NOPE + MLA_V)
    return kv[..., :MLA_NOPE], kv[..., MLA_NOPE:]


def mla_core(q_nope, q_rope, k_nope, k_rope, v, q_pos, k_pos):
    s = (jnp.einsum('bqhd,bkhd->bhqk', q_nope.astype(jnp.float32), k_nope.astype(jnp.float32))
         + jnp.einsum('bqhr,bkr->bhqk', q_rope.astype(jnp.float32), k_rope.astype(jnp.float32)))
    s = s * MLA_SCALE
    mask = (k_pos // CHUNK)[None, :] <= (q_pos // CHUNK)[:, None]
    p = jax.nn.softmax(jnp.where(mask[None, None], s, NEG), axis=-1)
    return jnp.einsum('bhqk,bkhe->bqhe', p, v.astype(jnp.float32)).astype(v.dtype)


def mla_prompt(q_nope, q_rope, k_nope, k_rope, v, pos):
    B, S = q_nope.shape[:2]
    nb = S // Q_BLOCK
    def blocks(t):
        return t.reshape(B, nb, Q_BLOCK, *t.shape[2:]).swapaxes(0, 1)
    def one_block(args):
        qn, qr, qp = args
        return mla_core(qn, qr, k_nope, k_rope, v, qp, pos)
    out = lax.map(one_block, (blocks(q_nope), blocks(q_rope), pos.reshape(nb, Q_BLOCK)))
    return out.swapaxes(0, 1).reshape(B, S, MLA_HEADS * MLA_V)


def retention_block(q, k, v, state, log_gamma):
    L = q.shape[1]
    qf, kf, vf = q.astype(jnp.float32), k.astype(jnp.float32), v.astype(jnp.float32)
    sf = state.astype(jnp.float32)
    idx = jnp.arange(L, dtype=jnp.float32)
    diff = idx[:, None] - idx[None, :]
    decay = jnp.where(diff >= 0, jnp.exp(jnp.maximum(diff, 0.0)[None] * log_gamma[:, None, None]), 0.0)
    inner = jnp.einsum('bnhd,bmhd->bhnm', qf, kf) * decay[None]
    o = jnp.einsum('bhnm,bmhe->bnhe', inner, vf)
    q_decay = jnp.exp((idx + 1.0)[:, None] * log_gamma[None, :])
    o = o + jnp.einsum('bnhd,bhde->bnhe', qf * q_decay[None, :, :, None], sf)
    k_decay = jnp.exp((L - 1.0 - idx)[:, None] * log_gamma[None, :])
    new_state = (jnp.exp(L * log_gamma)[None, :, None, None] * sf
                 + jnp.einsum('bmhd,bmhe->bhde', kf * k_decay[None, :, :, None], vf))
    return o, new_state


def retention_prompt(q, k, v, log_gamma):
    B, S = q.shape[:2]
    nc = S // CHUNK
    def chunks(t):
        return t.reshape(B, nc, CHUNK, *t.shape[2:]).swapaxes(0, 1)
    def step(st, c):
        qc, kc, vc = c
        o, st = retention_block(qc, kc, vc, st, log_gamma)
        return st, o
    st0 = jnp.zeros((B, RET_HEADS, RET_DK, RET_DV), jnp.float32)
    st, o = lax.scan(step, st0, (chunks(q), chunks(k), chunks(v)))
    return o.swapaxes(0, 1).reshape(B, S, RET_HEADS, RET_DV), st


def retention_out(o, g, gn_w):
    B, L = o.shape[:2]
    mu = jnp.mean(o, axis=-1, keepdims=True)
    var = jnp.mean(jnp.square(o - mu), axis=-1, keepdims=True)
    on = ((o - mu) * lax.rsqrt(var + EPS)).reshape(B, L, RET_HEADS * RET_DV) * gn_w.astype(jnp.float32)
    return (jax.nn.silu(g.astype(jnp.float32)) * on).astype(g.dtype)


def channel_and_ple(h, p_i, norm_ffn_w, w_ff1, w_ff2, norm_ple_w, w_ple_gate, w_ple_proj):
    u = rms_norm(h, norm_ffn_w) @ w_ff1
    h = h + jnp.square(jax.nn.relu(u)) @ w_ff2
    gate = jax.nn.sigmoid(rms_norm(h, norm_ple_w) @ w_ple_gate)
    return h + (p_i @ w_ple_proj) * gate


def setup_inputs(seed: int = 0) -> dict:
    key = jax.random.key(seed)
    ks = jax.random.split(key, 24)
    f32 = jnp.float32
    def nrm(k, shape, scale=1.0):
        return jax.random.normal(k, shape, f32) * scale
    def gain(k, shape):
        return 1.0 + 0.05 * jax.random.normal(k, shape, f32)
    return {
        'x_prompt': nrm(ks[0], (BATCH, SEQ, D_MODEL)),
        'x_sample': nrm(ks[1], (DEC_BATCH, DEC_SEQ, D_MODEL)),
        'cache_ckv': nrm(ks[2], (DEPTH, DEC_BATCH, PAST_LEN, MLA_KV_LORA)),
        'cache_krope': nrm(ks[3], (DEPTH, DEC_BATCH, PAST_LEN, MLA_ROPE)),
        'state_ret': nrm(ks[4], (DEPTH, DEC_BATCH, RET_HEADS, RET_DK, RET_DV)),
        'p_prompt': nrm(ks[5], (DEPTH, BATCH, SEQ, PLE_DIM)),
        'p_sample': nrm(ks[6], (DEPTH, DEC_BATCH, DEC_SEQ, PLE_DIM)),
        'norm_mix_w': gain(ks[7], (DEPTH, D_MODEL)),
        'w_in': nrm(ks[8], (DEPTH, D_MODEL, D_IN), D_MODEL ** -0.5),
        'q_norm_w': gain(ks[9], (DEPTH, MLA_Q_LORA)),
        'w_uq': nrm(ks[10], (DEPTH, MLA_Q_LORA, MLA_HEADS * (MLA_NOPE + MLA_ROPE)), MLA_Q_LORA ** -0.5),
        'kv_norm_w': gain(ks[11], (DEPTH, MLA_KV_LORA)),
        'w_ukv': nrm(ks[12], (DEPTH, MLA_KV_LORA, MLA_HEADS * (MLA_NOPE + MLA_V)), MLA_KV_LORA ** -0.5),
        'ret_gn_w': gain(ks[13], (DEPTH, RET_HEADS * RET_DV)),
        'w_out': nrm(ks[14], (DEPTH, D_MIX, D_MODEL), D_MIX ** -0.5),
        'norm_ffn_w': gain(ks[15], (DEPTH, D_MODEL)),
        'w_ff1': nrm(ks[16], (DEPTH, D_MODEL, D_FF), D_MODEL ** -0.5),
        'w_ff2': nrm(ks[17], (DEPTH, D_FF, D_MODEL), D_FF ** -0.5),
        'norm_ple_w': gain(ks[18], (DEPTH, D_MODEL)),
        'w_ple_gate': nrm(ks[19], (DEPTH, D_MODEL, D_MODEL), D_MODEL ** -0.5),
        'w_ple_proj': nrm(ks[20], (DEPTH, PLE_DIM, D_MODEL), PLE_DIM ** -0.5),
        'final_norm_w': gain(ks[21], (D_MODEL,)),
    }


def reference(x_prompt, x_sample, cache_ckv, cache_krope, state_ret, p_prompt, p_sample,
              norm_mix_w, w_in, q_norm_w, w_uq, kv_norm_w, w_ukv, ret_gn_w, w_out,
              norm_ffn_w, w_ff1, w_ff2, norm_ple_w, w_ple_gate, w_ple_proj, final_norm_w):
    log_gamma = jnp.log1p(-jnp.exp2(-5.0 - jnp.arange(RET_HEADS, dtype=jnp.float32)))
    B, S = x_prompt.shape[:2]
    Bd, Ld = x_sample.shape[:2]
    past = cache_ckv.shape[2]
    pos_p = jnp.arange(S, dtype=jnp.int32)
    pos_s = past + jnp.arange(Ld, dtype=jnp.int32)
    pos_all = jnp.arange(past + Ld, dtype=jnp.int32)

    hp, hs = x_prompt, x_sample
    ckv_p_l, kr_p_l, st_p_l, ckv_s_l, kr_s_l, st_s_l = [], [], [], [], [], []
    for i in range(DEPTH):
        a = rms_norm(hp, norm_mix_w[i])
        qn, qr, ckv, kr, rq, rk, rv, rg = project_mixers(a, pos_p, w_in[i], q_norm_w[i], w_uq[i], kv_norm_w[i])
        k_nope, v = expand_latent(ckv, w_ukv[i])
        att = mla_prompt(qn, qr, k_nope, kr, v, pos_p)
        ret, st = retention_prompt(rq, rk, rv, log_gamma)
        mix = jnp.concatenate([att, retention_out(ret, rg, ret_gn_w[i])], axis=-1)
        hp = hp + mix @ w_out[i]
        hp = channel_and_ple(hp, p_prompt[i], norm_ffn_w[i], w_ff1[i], w_ff2[i],
                             norm_ple_w[i], w_ple_gate[i], w_ple_proj[i])
        ckv_p_l.append(ckv)
        kr_p_l.append(kr)
        st_p_l.append(st.astype(x_prompt.dtype))

        a = rms_norm(hs, norm_mix_w[i])
        qn, qr, ckv, kr, rq, rk, rv, rg = project_mixers(a, pos_s, w_in[i], q_norm_w[i], w_uq[i], kv_norm_w[i])
        ckv_all = jnp.concatenate([cache_ckv[i], ckv], axis=1)
        kr_all = jnp.concatenate([cache_krope[i], kr], axis=1)
        k_nope, v = expand_latent(ckv_all, w_ukv[i])
        att = mla_core(qn, qr, k_nope, kr_all, v, pos_s, pos_all).reshape(Bd, Ld, MLA_HEADS * MLA_V)
        ret, st = retention_block(rq, rk, rv, state_ret[i], log_gamma)
        mix = jnp.concatenate([att, retention_out(ret, rg, ret_gn_w[i])], axis=-1)
        hs = hs + mix @ w_out[i]
        hs = channel_and_ple(hs, p_sample[i], norm_ffn_w[i], w_ff1[i], w_ff2[i],
                             norm_ple_w[i], w_ple_gate[i], w_ple_proj[i])
        ckv_s_l.append(ckv)
        kr_s_l.append(kr)
        st_s_l.append(st.astype(x_sample.dtype))

    y_prompt = rms_norm(hp, final_norm_w)
    y_sample = rms_norm(hs, final_norm_w)
    ckv_prompt = jnp.stack(ckv_p_l)
    krope_prompt = jnp.stack(kr_p_l)
    ret_prompt = jnp.stack(st_p_l)
    ckv_sample = jnp.stack(ckv_s_l)
    krope_sample = jnp.stack(kr_s_l)
    ret_sample = jnp.stack(st_s_l)
    return (y_prompt, y_sample, ckv_prompt, krope_prompt, ret_prompt, ckv_sample, krope_sample, ret_sample)
```

```python
import functools

import numpy as np
import jax
import jax.numpy as jnp
from jax import lax
from jax.experimental import pallas as pl
from jax.experimental.pallas import tpu as pltpu

D_MODEL = 1024
CHUNK = 64
PLE_DIM = 256
D_FF = 4 * D_MODEL
MLA_HEADS = 8
MLA_NOPE = 64
MLA_ROPE = 32
MLA_V = 64
MLA_Q_LORA = 384
MLA_KV_LORA = 256
MLA_SCALE = (MLA_NOPE + MLA_ROPE) ** -0.5
RET_HEADS = 8
RET_DK = 32
RET_DV = 64
ROPE_THETA = 10000.0
EPS = 1e-6
NEG = -1e30

LANES = 128
HEAD_PAIRS = MLA_HEADS // 2
VMEM_LIMIT = 56 * 1024 * 1024

_C_Q, _C_C, _C_R, _C_RV, _C_RG, _C_KR, _C_END = 0, 384, 640, 1664, 2176, 2688, 2944

F32 = jnp.float32
BF16 = jnp.bfloat16


def _rot_cols(base, nheads, hd):
    half = hd // 2
    cols, sign = [], []
    for h in range(nheads):
        for d in range(hd):
            cols.append(base + h * hd + (d + half if d < half else d - half))
            sign.append(-1.0 if d < half else 1.0)
    return cols, sign


def _in_proj_layout():
    o_q, o_c, o_kr, o_rq, o_rk, o_rv, o_rg = 0, 384, 640, 672, 928, 1184, 1696
    cols, mult = [], []

    def plain(base, n):
        cols.extend(range(base, base + n))
        mult.extend([1.0] * n)

    def slab(c, s):
        cols.extend([0] * 64 + list(c) + [0] * 32)
        mult.extend([0.0] * 64 + list(s) + [0.0] * 32)

    plain(o_q, MLA_Q_LORA)
    plain(o_c, MLA_KV_LORA)
    plain(o_rq, 256)
    plain(o_rk, 256)
    for base in (o_rq, o_rk):
        c, s = _rot_cols(base, RET_HEADS, RET_DK)
        cols.extend(c)
        mult.extend(s)
    plain(o_rv, 512)
    plain(o_rg, 512)
    slab(range(o_kr, o_kr + MLA_ROPE), [1.0] * MLA_ROPE)
    slab(*_rot_cols(o_kr, 1, MLA_ROPE))
    assert len(cols) == _C_END
    return np.asarray(cols, np.int32), np.asarray(mult, np.float32)


def _uq_layout():
    hd = MLA_NOPE + MLA_ROPE
    cols, mult, rcols, rmult = [], [], [], []
    for h in range(MLA_HEADS):
        cols.extend(list(range(h * hd, h * hd + hd)) + [0] * 32)
        mult.extend([1.0] * hd + [0.0] * 32)
        c, s = _rot_cols(h * hd + MLA_NOPE, 1, MLA_ROPE)
        rcols.extend([0] * 64 + c + [0] * 32)
        rmult.extend([0.0] * 64 + s + [0.0] * 32)
    return (np.asarray(cols, np.int32), np.asarray(mult, np.float32),
            np.asarray(rcols, np.int32), np.asarray(rmult, np.float32))


def _ukv_layout():
    hd = MLA_NOPE + MLA_V
    kcols, kmult, vcols = [], [], []
    for h in range(MLA_HEADS):
        kcols.extend(list(range(h * hd, h * hd + MLA_NOPE)) + [0] * 64)
        kmult.extend([1.0] * MLA_NOPE + [0.0] * 64)
        vcols.extend(range(h * hd + MLA_NOPE, (h + 1) * hd))
    return np.asarray(kcols, np.int32), np.asarray(kmult, np.float32), np.asarray(vcols, np.int32)


def _take_cols(w, cols, mult):
    return (jnp.take(w, jnp.asarray(cols), axis=1) * jnp.asarray(mult)[None, :]).astype(BF16)


def _layer_weights(w_in, w_uq, w_ukv, w_out, w_ff1, w_ff2, w_ple_gate, w_ple_proj):
    ic, im = _in_proj_layout()
    qc, qm, qrc, qrm = _uq_layout()
    kc, km, vc = _ukv_layout()
    place = np.zeros((MLA_ROPE, MLA_HEADS * LANES), np.float32)
    for h in range(MLA_HEADS):
        for r in range(MLA_ROPE):
            place[r, h * LANES + MLA_NOPE + r] = 1.0
    return dict(
        win=_take_cols(w_in, ic, im),
        wq=_take_cols(w_uq, qc, qm),
        wqr=_take_cols(w_uq, qrc, qrm),
        wk=_take_cols(w_ukv, kc, km),
        wv=jnp.take(w_ukv, jnp.asarray(vc), axis=1).astype(BF16),
        place=jnp.asarray(place, BF16),
        woa=w_out[:MLA_HEADS * MLA_V].astype(BF16),
        wor=w_out[MLA_HEADS * MLA_V:].astype(BF16),
        w1=w_ff1.astype(BF16),
        w2=w_ff2.astype(BF16),
        wg=w_ple_gate.astype(BF16),
        wp=w_ple_proj.astype(BF16),
    )


def _rope_table(pos):
    half = MLA_ROPE // 2
    inv = ROPE_THETA ** (-jnp.arange(half, dtype=F32) / half)
    ang = pos.astype(F32)[:, None] * inv[None, :]
    cos, sin = jnp.cos(ang), jnp.sin(ang)
    return jnp.stack([jnp.tile(cos, (1, LANES // half)), jnp.tile(sin, (1, LANES // half))])


def _retention_tables(L):
    log_gamma = jnp.log1p(-jnp.exp2(-5.0 - jnp.arange(RET_HEADS, dtype=F32)))
    idx = jnp.arange(L, dtype=F32)
    diff = idx[:, None] - idx[None, :]
    dec = jnp.where(diff >= 0, jnp.exp(jnp.maximum(diff, 0.0)[None] * log_gamma[:, None, None]), 0.0)
    lg_lane = jnp.repeat(log_gamma, RET_DK).reshape(2, 1, LANES)
    qd = jnp.exp((idx + 1.0)[None, :, None] * lg_lane)
    kd = jnp.exp((L - 1.0 - idx)[None, :, None] * lg_lane)
    rs = jnp.broadcast_to(jnp.exp(L * lg_lane).reshape(2, LANES, 1), (2, LANES, LANES))
    return dec, qd, kd, rs


def _state_to_slabs(state):
    B = state.shape[0]
    t = jnp.zeros((B, HEAD_PAIRS, 4, RET_DK, 2, RET_DV), F32)
    for h in range(RET_HEADS):
        t = t.at[:, h // 2, h % 4, :, h % 2, :].set(state[:, h].astype(F32))
    return t.reshape(B, HEAD_PAIRS, LANES, LANES)


def _slabs_to_state(t):
    B = t.shape[0]
    t = t.reshape(B, HEAD_PAIRS, 4, RET_DK, 2, RET_DV)
    return jnp.stack([t[:, h // 2, h % 4, :, h % 2, :] for h in range(RET_HEADS)], axis=1)


def _rms(x, w):
    return x * lax.rsqrt(jnp.mean(x * x, axis=-1, keepdims=True) + EPS) * w


def _dot(a, b):
    return jnp.dot(a, b, preferred_element_type=F32)


def _dot_nt(a, b):
    return lax.dot_general(a, b, (((1,), (1,)), ((), ())), preferred_element_type=F32)


def _dot_tn(a, b):
    return lax.dot_general(a, b, (((0,), (0,)), ((), ())), preferred_element_type=F32)


def _const_spec(shape):
    zeros = (0,) * len(shape)
    return pl.BlockSpec(shape, lambda *_: zeros, pipeline_mode=pl.Buffered(1))


def _params(*sem):
    return pltpu.CompilerParams(dimension_semantics=sem, vmem_limit_bytes=VMEM_LIMIT)


def _proj_kernel(h_ref, tab_ref, nw_ref, win_ref, qnw_ref, wq_ref, wqr_ref, kvnw_ref, wk_ref, wv_ref,
                 q_ref, k_ref, v_ref, ckv_ref, kr_ref, rq_ref, rk_ref, rv_ref, rg_ref):
    tm = h_ref.shape[0]
    cos, sin = tab_ref[0], tab_ref[1]
    lane = lax.broadcasted_iota(jnp.int32, (tm, LANES), 1)
    rope_lanes = (lane >= MLA_NOPE) & (lane < MLA_NOPE + MLA_ROPE)
    a = _rms(h_ref[...], nw_ref[...]).astype(BF16)

    ql = _rms(_dot(a, win_ref[:, _C_Q:_C_C]), qnw_ref[...]).astype(BF16)
    qz, qzr = _dot(ql, wq_ref[...]), _dot(ql, wqr_ref[...])
    cq = jnp.where(lane < MLA_NOPE, MLA_SCALE, jnp.where(rope_lanes, cos * MLA_SCALE, 0.0))
    sq = jnp.where(rope_lanes, sin * MLA_SCALE, 0.0)
    for h in range(MLA_HEADS):
        sl = slice(h * LANES, (h + 1) * LANES)
        q_ref[:, sl] = (qz[:, sl] * cq + qzr[:, sl] * sq).astype(BF16)

    c = _rms(_dot(a, win_ref[:, _C_C:_C_R]), kvnw_ref[...])
    ckv_ref[...] = c
    cb = c.astype(BF16)
    v_ref[...] = _dot(cb, wv_ref[...]).astype(BF16)
    zkr = _dot(a, win_ref[:, _C_KR:_C_END])
    kslab = jnp.where(rope_lanes, zkr[:, :LANES] * cos + zkr[:, LANES:] * sin, 0.0)
    kr_ref[...] = kslab[:, MLA_NOPE:MLA_NOPE + MLA_ROPE]
    kn = _dot(cb, wk_ref[...])
    for h in range(MLA_HEADS):
        sl = slice(h * LANES, (h + 1) * LANES)
        k_ref[:, sl] = (kn[:, sl] + kslab).astype(BF16)

    zr = _dot(a, win_ref[:, _C_R:_C_RV])
    kscale = RET_DK ** -0.5
    for s in range(2):
        sl = slice(s * LANES, (s + 1) * LANES)
        rq_ref[:, sl] = (zr[:, s * LANES:(s + 1) * LANES] * cos
                         + zr[:, 512 + s * LANES:512 + (s + 1) * LANES] * sin).astype(BF16)
        rk_ref[:, sl] = ((zr[:, 256 + s * LANES:256 + (s + 1) * LANES] * cos
                          + zr[:, 768 + s * LANES:768 + (s + 1) * LANES] * sin) * kscale).astype(BF16)
    rv_ref[...] = _dot(a, win_ref[:, _C_RV:_C_RG]).astype(BF16)
    g = _dot(a, win_ref[:, _C_RG:_C_KR])
    rg_ref[...] = (g * jax.nn.sigmoid(g)).astype(BF16)


def _proj(h, tab, norm_w, q_norm_w, kv_norm_w, w, *, tm, n_pos, n_rep):
    N = h.shape[0]
    row = lambda s, b: (b * n_pos + s, 0)
    outs = [(MLA_HEADS * LANES, BF16), (MLA_HEADS * LANES, BF16), (MLA_HEADS * MLA_V, BF16),
            (MLA_KV_LORA, F32), (MLA_ROPE, F32), (256, BF16), (256, BF16), (512, BF16), (512, BF16)]
    return pl.pallas_call(
        _proj_kernel,
        grid=(n_pos, n_rep),
        in_specs=[
            pl.BlockSpec((tm, D_MODEL), row),
            pl.BlockSpec((2, tm, LANES), lambda s, b: (0, s, 0)),
            _const_spec((1, D_MODEL)), _const_spec((D_MODEL, _C_END)),
            _const_spec((1, MLA_Q_LORA)), _const_spec(w['wq'].shape), _const_spec(w['wqr'].shape),
            _const_spec((1, MLA_KV_LORA)), _const_spec(w['wk'].shape), _const_spec(w['wv'].shape),
        ],
        out_specs=[pl.BlockSpec((tm, c), row) for c, _ in outs],
        out_shape=[jax.ShapeDtypeStruct((N, c), dt) for c, dt in outs],
        compiler_params=_params("arbitrary", "arbitrary"),
        name="proj",
    )(h, tab, norm_w.reshape(1, -1), w['win'], q_norm_w.reshape(1, -1), w['wq'], w['wqr'],
      kv_norm_w.reshape(1, -1), w['wk'], w['wv'])


def _expand_kernel(c_ref, kr_ref, wk_ref, place_ref, wv_ref, k_ref, v_ref):
    cb = c_ref[...].astype(BF16)
    k_ref[...] = (_dot(cb, wk_ref[...]) + _dot(kr_ref[...].astype(BF16), place_ref[...])).astype(BF16)
    v_ref[...] = _dot(cb, wv_ref[...]).astype(BF16)


def _expand(ckv, krope, w, *, tm):
    N = ckv.shape[0]
    row = lambda i: (i, 0)
    return pl.pallas_call(
        _expand_kernel,
        grid=(N // tm,),
        in_specs=[pl.BlockSpec((tm, MLA_KV_LORA), row), pl.BlockSpec((tm, MLA_ROPE), row),
                  _const_spec(w['wk'].shape), _const_spec(w['place'].shape), _const_spec(w['wv'].shape)],
        out_specs=[pl.BlockSpec((tm, MLA_HEADS * LANES), row), pl.BlockSpec((tm, MLA_HEADS * MLA_V), row)],
        out_shape=[jax.ShapeDtypeStruct((N, MLA_HEADS * LANES), BF16),
                   jax.ShapeDtypeStruct((N, MLA_HEADS * MLA_V), BF16)],
        compiler_params=_params("arbitrary"),
        name="expand",
    )(ckv, krope, w['wk'], w['place'], w['wv'])


def _softmax_step(carry, s, vt):
    m, l, acc = carry
    m_new = jnp.maximum(m, jnp.max(s, axis=-1, keepdims=True))
    alpha = jnp.exp(m - m_new)
    p = jnp.exp(s - m_new)
    l = alpha * l + jnp.sum(p, axis=-1, keepdims=True)
    acc = alpha * acc + _dot(p.astype(BF16), vt)
    return m_new, l, acc


def _attn_prompt_kernel(q_ref, k_ref, v_ref, o_ref, *, t):
    i = pl.program_id(2)
    qs = [q_ref[:, hh * LANES:(hh + 1) * LANES] for hh in range(2)]

    def tile(j, carries, masked):
        start = pl.multiple_of(j * t, t)
        vt = v_ref[pl.ds(start, t), :]
        out = []
        for hh in range(2):
            s = _dot_nt(qs[hh], k_ref[pl.ds(start, t), hh * LANES:(hh + 1) * LANES])
            if masked:
                rc = lax.broadcasted_iota(jnp.int32, (t, t), 0) // CHUNK
                cc = lax.broadcasted_iota(jnp.int32, (t, t), 1) // CHUNK
                s = jnp.where(cc <= rc, s, NEG)
            out.append(_softmax_step(carries[hh], s, vt))
        return tuple(out)

    init = tuple((jnp.full((t, 1), NEG, F32), jnp.zeros((t, 1), F32), jnp.zeros((t, LANES), F32))
                 for _ in range(2))
    carries = lax.fori_loop(0, i, lambda j, c: tile(j, c, False), init)
    (_, l0, a0), (_, l1, a1) = tile(i, carries, True)
    lane = lax.broadcasted_iota(jnp.int32, (t, LANES), 1)
    o_ref[...] = jnp.where(lane < MLA_V, a0 / l0, a1 / l1).astype(BF16)


def _attn_prompt(q, k, v, *, t):
    B, S, _ = q.shape
    return pl.pallas_call(
        functools.partial(_attn_prompt_kernel, t=t),
        grid=(B, HEAD_PAIRS, S // t),
        in_specs=[pl.BlockSpec((None, t, 2 * LANES), lambda b, p, i: (b, i, p)),
                  pl.BlockSpec((None, S, 2 * LANES), lambda b, p, i: (b, 0, p)),
                  pl.BlockSpec((None, S, LANES), lambda b, p, i: (b, 0, p))],
        out_specs=pl.BlockSpec((None, t, LANES), lambda b, p, i: (b, i, p)),
        out_shape=jax.ShapeDtypeStruct((B, S, MLA_HEADS * MLA_V), BF16),
        compiler_params=_params("arbitrary", "arbitrary", "arbitrary"),
        name="attn_prompt",
    )(q, k, v)


def _attn_sample_kernel(q_ref, kc_ref, vc_ref, kn_ref, vn_ref, o_ref):
    L = q_ref.shape[0]
    outs = []
    for hh in range(2):
        sl = slice(hh * LANES, (hh + 1) * LANES)
        qh = q_ref[:, sl]
        sc = _dot_nt(qh, kc_ref[:, sl])
        sn = _dot_nt(qh, kn_ref[:, sl])
        m = jnp.maximum(jnp.max(sc, axis=-1, keepdims=True), jnp.max(sn, axis=-1, keepdims=True))
        pc, pn = jnp.exp(sc - m), jnp.exp(sn - m)
        l = jnp.sum(pc, axis=-1, keepdims=True) + jnp.sum(pn, axis=-1, keepdims=True)
        acc = _dot(pc.astype(BF16), vc_ref[...]) + _dot(pn.astype(BF16), vn_ref[...])
        outs.append(acc / l)
    lane = lax.broadcasted_iota(jnp.int32, (L, LANES), 1)
    o_ref[...] = jnp.where(lane < MLA_V, outs[0], outs[1]).astype(BF16)


def _attn_sample(q, kc, vc, kn, vn):
    B, L, _ = q.shape
    P = kc.shape[1]
    return pl.pallas_call(
        _attn_sample_kernel,
        grid=(B, HEAD_PAIRS),
        in_specs=[pl.BlockSpec((None, L, 2 * LANES), lambda b, p: (b, 0, p)),
                  pl.BlockSpec((None, P, 2 * LANES), lambda b, p: (b, 0, p)),
                  pl.BlockSpec((None, P, LANES), lambda b, p: (b, 0, p)),
                  pl.BlockSpec((None, L, 2 * LANES), lambda b, p: (b, 0, p)),
                  pl.BlockSpec((None, L, LANES), lambda b, p: (b, 0, p))],
        out_specs=pl.BlockSpec((None, L, LANES), lambda b, p: (b, 0, p)),
        out_shape=jax.ShapeDtypeStruct((B, L, MLA_HEADS * MLA_V), BF16),
        compiler_params=_params("arbitrary", "arbitrary"),
        name="attn_sample",
    )(q, kc, vc, kn, vn)


def _ret_kernel(rq_ref, rk_ref, rv_ref, g_ref, t0_ref, dec_ref, qd_ref, kd_ref, rs_ref, gnw_ref,
                o_ref, st_ref):
    L = rq_ref.shape[0]

    @pl.when(pl.program_id(1) == 0)
    def _():
        st_ref[...] = t0_ref[...]

    lane = lax.broadcasted_iota(jnp.int32, (L, LANES), 1)
    low = lane < RET_DV
    srow = lax.broadcasted_iota(jnp.int32, (LANES, LANES), 0) // RET_DK
    scol = lax.broadcasted_iota(jnp.int32, (LANES, LANES), 1) // RET_DV
    for p in range(HEAD_PAIRS):
        slab = p // 2
        sl_qk = slice(slab * LANES, (slab + 1) * LANES)
        sl_v = slice(p * LANES, (p + 1) * LANES)
        qs, ks, vp = rq_ref[:, sl_qk], rk_ref[:, sl_qk], rv_ref[:, sl_v]
        state = st_ref[p]
        qdec = (qs.astype(F32) * qd_ref[slab]).astype(BF16)
        o = _dot(qdec, state.astype(BF16))
        for c in range(2):
            h = 2 * p + c
            qm = jnp.where(lane // RET_DK == h % 4, qs, jnp.zeros_like(qs))
            inner = _dot_nt(qm, ks) * dec_ref[h]
            oh = _dot(inner.astype(BF16), vp)
            o = o + jnp.where(low if c == 0 else ~low, oh, 0.0)
        kdec = (ks.astype(F32) * kd_ref[slab]).astype(BF16)
        new_state = rs_ref[slab] * state + _dot_tn(kdec, vp)
        st_ref[p] = jnp.where(srow == 2 * (p % 2) + scol, new_state, 0.0)

        def half_mean(x):
            lo = jnp.sum(jnp.where(low, x, 0.0), axis=-1, keepdims=True)
            hi = jnp.sum(jnp.where(low, 0.0, x), axis=-1, keepdims=True)
            return jnp.where(low, lo, hi) * (1.0 / RET_DV)
        d = o - half_mean(o)
        on = d * lax.rsqrt(half_mean(d * d) + EPS) * gnw_ref[:, sl_v]
        o_ref[:, sl_v] = (g_ref[:, sl_v].astype(F32) * on).astype(BF16)


def _retention(rq, rk, rv, g, t0, gn_w, *, L):
    B, S, _ = rq.shape
    dec, qd, kd, rs = _retention_tables(L)
    blk = lambda b, i: (b, i, 0)
    st = lambda b, i: (b, 0, 0, 0)
    return pl.pallas_call(
        _ret_kernel,
        grid=(B, S // L),
        in_specs=[pl.BlockSpec((None, L, 256), blk), pl.BlockSpec((None, L, 256), blk),
                  pl.BlockSpec((None, L, 512), blk), pl.BlockSpec((None, L, 512), blk),
                  pl.BlockSpec((None, HEAD_PAIRS, LANES, LANES), st),
                  _const_spec(dec.shape), _const_spec(qd.shape), _const_spec(kd.shape),
                  _const_spec(rs.shape), _const_spec((1, 512))],
        out_specs=[pl.BlockSpec((None, L, 512), blk),
                   pl.BlockSpec((None, HEAD_PAIRS, LANES, LANES), st)],
        out_shape=[jax.ShapeDtypeStruct((B, S, 512), BF16),
                   jax.ShapeDtypeStruct((B, HEAD_PAIRS, LANES, LANES), F32)],
        compiler_params=_params("arbitrary", "arbitrary"),
        name="retention",
    )(rq, rk, rv, g, t0, dec, qd, kd, rs, gn_w.reshape(1, -1))


def _post_kernel(h_ref, att_ref, ret_ref, p_ref, woa_ref, wor_ref, nf_ref, w1_ref, w2_ref,
                 np_ref, wg_ref, wp_ref, fin_ref, o_ref, *, final, fc):
    h = h_ref[...] + _dot(att_ref[...], woa_ref[...]) + _dot(ret_ref[...], wor_ref[...])
    a = _rms(h, nf_ref[...]).astype(BF16)
    acc = jnp.zeros_like(h)
    for c in range(D_FF // fc):
        u = _dot(a, w1_ref[:, c * fc:(c + 1) * fc])
        acc = acc + _dot(jnp.square(jnp.maximum(u, 0.0)).astype(BF16), w2_ref[c * fc:(c + 1) * fc, :])
    h = h + acc
    gate = jax.nn.sigmoid(_dot(_rms(h, np_ref[...]).astype(BF16), wg_ref[...]))
    h = h + _dot(p_ref[...].astype(BF16), wp_ref[...]) * gate
    o_ref[...] = _rms(h, fin_ref[...]) if final else h


def _post(h, att, ret, p, norm_ffn_w, norm_ple_w, final_norm_w, w, *, tm, final):
    N = h.shape[0]
    row = lambda i: (i, 0)
    return pl.pallas_call(
        functools.partial(_post_kernel, final=final, fc=512),
        grid=(N // tm,),
        in_specs=[pl.BlockSpec((tm, D_MODEL), row), pl.BlockSpec((tm, 512), row),
                  pl.BlockSpec((tm, 512), row), pl.BlockSpec((tm, PLE_DIM), row),
                  _const_spec(w['woa'].shape), _const_spec(w['wor'].shape), _const_spec((1, D_MODEL)),
                  _const_spec(w['w1'].shape), _const_spec(w['w2'].shape), _const_spec((1, D_MODEL)),
                  _const_spec(w['wg'].shape), _const_spec(w['wp'].shape), _const_spec((1, D_MODEL))],
        out_specs=pl.BlockSpec((tm, D_MODEL), row),
        out_shape=jax.ShapeDtypeStruct((N, D_MODEL), F32),
        compiler_params=_params("arbitrary"),
        name="post",
    )(h, att, ret, p, w['woa'], w['wor'], norm_ffn_w.reshape(1, -1), w['w1'], w['w2'],
      norm_ple_w.reshape(1, -1), w['wg'], w['wp'], final_norm_w.reshape(1, -1))


PROMPT_TM = 512
ATTN_T = 256
RET_BLOCK = 256
EXPAND_TM = 1024


def kernel(x_prompt, x_sample, cache_ckv, cache_krope, state_ret, p_prompt, p_sample, norm_mix_w, w_in,
           q_norm_w, w_uq, kv_norm_w, w_ukv, ret_gn_w, w_out, norm_ffn_w, w_ff1, w_ff2, norm_ple_w,
           w_ple_gate, w_ple_proj, final_norm_w):
    B, S, D = x_prompt.shape
    Bd, Ld, _ = x_sample.shape
    depth, _, past, _ = cache_ckv.shape
    assert past % CHUNK == 0 and Ld <= CHUNK and S % ATTN_T == 0 and ATTN_T % CHUNK == 0
    tm_p = min(PROMPT_TM, S)
    tm_c = min(EXPAND_TM, Bd * past)

    tab_p = _rope_table(jnp.arange(S, dtype=jnp.int32))
    tab_s = jnp.tile(_rope_table(past + jnp.arange(Ld, dtype=jnp.int32)), (1, Bd, 1))
    zero_state = jnp.zeros((B, HEAD_PAIRS, LANES, LANES), F32)

    hp = x_prompt.reshape(B * S, D)
    hs = x_sample.reshape(Bd * Ld, D)
    outs = [[] for _ in range(6)]
    for i in range(depth):
        w = _layer_weights(w_in[i], w_uq[i], w_ukv[i], w_out[i], w_ff1[i], w_ff2[i],
                           w_ple_gate[i], w_ple_proj[i])
        final = i == depth - 1

        q, k, v, ckv, kr, rq, rk, rv, rg = _proj(hp, tab_p, norm_mix_w[i], q_norm_w[i], kv_norm_w[i], w,
                                                 tm=tm_p, n_pos=S // tm_p, n_rep=B)
        att = _attn_prompt(q.reshape(B, S, -1), k.reshape(B, S, -1), v.reshape(B, S, -1), t=ATTN_T)
        ret, st = _retention(rq.reshape(B, S, -1), rk.reshape(B, S, -1), rv.reshape(B, S, -1),
                             rg.reshape(B, S, -1), zero_state, ret_gn_w[i], L=RET_BLOCK)
        hp = _post(hp, att.reshape(B * S, -1), ret.reshape(B * S, -1), p_prompt[i].reshape(B * S, -1),
                   norm_ffn_w[i], norm_ple_w[i], final_norm_w, w, tm=tm_p, final=final)
        outs[0].append(ckv.reshape(B, S, -1))
        outs[1].append(kr.reshape(B, S, -1))
        outs[2].append(_slabs_to_state(st).astype(x_prompt.dtype))

        q, k, v, ckv, kr, rq, rk, rv, rg = _proj(hs, tab_s, norm_mix_w[i], q_norm_w[i], kv_norm_w[i], w,
                                                 tm=Bd * Ld, n_pos=1, n_rep=1)
        kc, vc = _expand(cache_ckv[i].reshape(Bd * past, -1), cache_krope[i].reshape(Bd * past, -1), w,
                         tm=tm_c)
        att = _attn_sample(q.reshape(Bd, Ld, -1), kc.reshape(Bd, past, -1), vc.reshape(Bd, past, -1),
                           k.reshape(Bd, Ld, -1), v.reshape(Bd, Ld, -1))
        ret, st = _retention(rq.reshape(Bd, Ld, -1), rk.reshape(Bd, Ld, -1), rv.reshape(Bd, Ld, -1),
                             rg.reshape(Bd, Ld, -1), _state_to_slabs(state_ret[i]), ret_gn_w[i], L=Ld)
        hs = _post(hs, att.reshape(Bd * Ld, -1), ret.reshape(Bd * Ld, -1),
                   p_sample[i].reshape(Bd * Ld, -1), norm_ffn_w[i], norm_ple_w[i], final_norm_w, w,
                   tm=Bd * Ld, final=final)
        outs[3].append(ckv.reshape(Bd, Ld, -1))
        outs[4].append(kr.reshape(Bd, Ld, -1))
        outs[5].append(_slabs_to_state(st).astype(x_sample.dtype))

    return (hp.reshape(B, S, D), hs.reshape(Bd, Ld, D), jnp.stack(outs[0]), jnp.stack(outs[1]),
            jnp.stack(outs[2]), jnp.stack(outs[3]), jnp.stack(outs[4]), jnp.stack(outs[5]))
```

```python
import functools

import numpy as np
import jax
import jax.numpy as jnp
from jax import lax
from jax.experimental import pallas as pl
from jax.experimental.pallas import tpu as pltpu

D_MODEL = 1024
CHUNK = 64
PLE_DIM = 256
D_FF = 4 * D_MODEL
MLA_HEADS = 8
MLA_NOPE = 64
MLA_ROPE = 32
MLA_V = 64
MLA_Q_LORA = 384
MLA_KV_LORA = 256
MLA_SCALE = (MLA_NOPE + MLA_ROPE) ** -0.5
Q_SCALE = MLA_SCALE * float(np.log2(np.e))
RET_HEADS = 8
RET_DK = 32
RET_DV = 64
ROPE_THETA = 10000.0
EPS = 1e-6
NEG = -1e30

LANES = 128
HEAD_PAIRS = MLA_HEADS // 2
VMEM_LIMIT = 56 * 1024 * 1024

_C_Q, _C_C, _C_R, _C_RV, _C_RG, _C_KR, _C_END = 0, 384, 640, 1664, 2176, 2688, 2944

F32 = jnp.float32
BF16 = jnp.bfloat16


def _rot_cols(base, nheads, hd):
    half = hd // 2
    cols, sign = [], []
    for h in range(nheads):
        for d in range(hd):
            cols.append(base + h * hd + (d + half if d < half else d - half))
            sign.append(-1.0 if d < half else 1.0)
    return cols, sign


def _in_proj_layout():
    o_q, o_c, o_kr, o_rq, o_rk, o_rv, o_rg = 0, 384, 640, 672, 928, 1184, 1696
    cols, mult = [], []

    def plain(base, n):
        cols.extend(range(base, base + n))
        mult.extend([1.0] * n)

    def slab(c, s):
        cols.extend([0] * 64 + list(c) + [0] * 32)
        mult.extend([0.0] * 64 + list(s) + [0.0] * 32)

    plain(o_q, MLA_Q_LORA)
    plain(o_c, MLA_KV_LORA)
    plain(o_rq, 256)
    plain(o_rk, 256)
    for base in (o_rq, o_rk):
        c, s = _rot_cols(base, RET_HEADS, RET_DK)
        cols.extend(c)
        mult.extend(s)
    plain(o_rv, 512)
    plain(o_rg, 512)
    slab(range(o_kr, o_kr + MLA_ROPE), [1.0] * MLA_ROPE)
    slab(*_rot_cols(o_kr, 1, MLA_ROPE))
    assert len(cols) == _C_END
    return np.asarray(cols, np.int32), np.asarray(mult, np.float32)


def _uq_layout():
    hd = MLA_NOPE + MLA_ROPE
    cols, mult, rcols, rmult = [], [], [], []
    for h in range(MLA_HEADS):
        cols.extend(list(range(h * hd, h * hd + hd)) + [0] * 32)
        mult.extend([1.0] * hd + [0.0] * 32)
        c, s = _rot_cols(h * hd + MLA_NOPE, 1, MLA_ROPE)
        rcols.extend([0] * 64 + c + [0] * 32)
        rmult.extend([0.0] * 64 + s + [0.0] * 32)
    return (np.asarray(cols, np.int32), np.asarray(mult, np.float32),
            np.asarray(rcols, np.int32), np.asarray(rmult, np.float32))


def _ukv_layout():
    hd = MLA_NOPE + MLA_V
    kcols, kmult, vcols = [], [], []
    for h in range(MLA_HEADS):
        kcols.extend(list(range(h * hd, h * hd + MLA_NOPE)) + [0] * 64)
        kmult.extend([1.0] * MLA_NOPE + [0.0] * 64)
        vcols.extend(range(h * hd + MLA_NOPE, (h + 1) * hd))
    return np.asarray(kcols, np.int32), np.asarray(kmult, np.float32), np.asarray(vcols, np.int32)


def _take_cols(w, cols, mult):
    return (jnp.take(w, jnp.asarray(cols), axis=1) * jnp.asarray(mult)[None, :]).astype(BF16)


def _layer_weights(w_in, w_uq, w_ukv, w_out, w_ff1, w_ff2, w_ple_gate, w_ple_proj):
    ic, im = _in_proj_layout()
    qc, qm, qrc, qrm = _uq_layout()
    kc, km, vc = _ukv_layout()
    place = np.zeros((MLA_ROPE, MLA_HEADS * LANES), np.float32)
    for h in range(MLA_HEADS):
        for r in range(MLA_ROPE):
            place[r, h * LANES + MLA_NOPE + r] = 1.0
    return dict(
        win=_take_cols(w_in, ic, im),
        wq=_take_cols(w_uq, qc, qm),
        wqr=_take_cols(w_uq, qrc, qrm),
        wk=_take_cols(w_ukv, kc, km),
        wv=jnp.take(w_ukv, jnp.asarray(vc), axis=1).astype(BF16),
        place=jnp.asarray(place, BF16),
        woa=w_out[:MLA_HEADS * MLA_V].astype(BF16),
        wor=w_out[MLA_HEADS * MLA_V:].astype(BF16),
        w1=w_ff1.astype(BF16),
        w2=w_ff2.astype(BF16),
        wg=w_ple_gate.astype(BF16),
        wp=w_ple_proj.astype(BF16),
    )


def _rope_table(pos):
    half = MLA_ROPE // 2
    inv = ROPE_THETA ** (-jnp.arange(half, dtype=F32) / half)
    ang = pos.astype(F32)[:, None] * inv[None, :]
    cos, sin = jnp.cos(ang), jnp.sin(ang)
    return jnp.stack([jnp.tile(cos, (1, LANES // half)), jnp.tile(sin, (1, LANES // half))])


def _retention_tables(L):
    log_gamma = jnp.log1p(-jnp.exp2(-5.0 - jnp.arange(RET_HEADS, dtype=F32)))
    idx = jnp.arange(L, dtype=F32)
    diff = idx[:, None] - idx[None, :]
    dec = jnp.where(diff >= 0, jnp.exp(jnp.maximum(diff, 0.0)[None] * log_gamma[:, None, None]), 0.0)
    lg_lane = jnp.repeat(log_gamma, RET_DK).reshape(2, 1, LANES)
    qd = jnp.exp((idx + 1.0)[None, :, None] * lg_lane)
    kd = jnp.exp((L - 1.0 - idx)[None, :, None] * lg_lane)
    rs = jnp.broadcast_to(jnp.exp(L * lg_lane).reshape(2, LANES, 1), (2, LANES, LANES))
    return dec, qd, kd, rs


def _state_to_slabs(state):
    B = state.shape[0]
    t = jnp.zeros((B, HEAD_PAIRS, 4, RET_DK, 2, RET_DV), F32)
    for h in range(RET_HEADS):
        t = t.at[:, h // 2, h % 4, :, h % 2, :].set(state[:, h].astype(F32))
    return t.reshape(B, HEAD_PAIRS, LANES, LANES)


def _slabs_to_state(t):
    B = t.shape[0]
    t = t.reshape(B, HEAD_PAIRS, 4, RET_DK, 2, RET_DV)
    return jnp.stack([t[:, h // 2, h % 4, :, h % 2, :] for h in range(RET_HEADS)], axis=1)


def _rms(x, w):
    return x * lax.rsqrt(jnp.mean(x * x, axis=-1, keepdims=True) + EPS) * w


def _dot(a, b):
    return jnp.dot(a, b, preferred_element_type=F32)


def _dot_nt(a, b):
    return lax.dot_general(a, b, (((1,), (1,)), ((), ())), preferred_element_type=F32)


def _dot_tn(a, b):
    return lax.dot_general(a, b, (((0,), (0,)), ((), ())), preferred_element_type=F32)


def _const_spec(shape):
    zeros = (0,) * len(shape)
    return pl.BlockSpec(shape, lambda *_: zeros, pipeline_mode=pl.Buffered(1))


def _params(*sem):
    return pltpu.CompilerParams(dimension_semantics=sem, vmem_limit_bytes=VMEM_LIMIT)


def _proj_kernel(h_ref, tab_ref, nw_ref, win_ref, qnw_ref, wq_ref, wqr_ref, kvnw_ref, wk_ref, wv_ref,
                 q_ref, k_ref, v_ref, ckv_ref, kr_ref, rq_ref, rk_ref, rv_ref, rg_ref):
    tm = h_ref.shape[0]
    cos, sin = tab_ref[0], tab_ref[1]
    lane = lax.broadcasted_iota(jnp.int32, (tm, LANES), 1)
    rope_lanes = (lane >= MLA_NOPE) & (lane < MLA_NOPE + MLA_ROPE)
    a = _rms(h_ref[...], nw_ref[...]).astype(BF16)

    ql = _rms(_dot(a, win_ref[:, _C_Q:_C_C]), qnw_ref[...]).astype(BF16)
    qz, qzr = _dot(ql, wq_ref[...]), _dot(ql, wqr_ref[...])
    cq = jnp.where(lane < MLA_NOPE, Q_SCALE, jnp.where(rope_lanes, cos * Q_SCALE, 0.0))
    sq = jnp.where(rope_lanes, sin * Q_SCALE, 0.0)
    for h in range(MLA_HEADS):
        sl = slice(h * LANES, (h + 1) * LANES)
        q_ref[:, sl] = (qz[:, sl] * cq + qzr[:, sl] * sq).astype(BF16)

    c = _rms(_dot(a, win_ref[:, _C_C:_C_R]), kvnw_ref[...])
    ckv_ref[...] = c
    cb = c.astype(BF16)
    v_ref[...] = _dot(cb, wv_ref[...]).astype(BF16)
    zkr = _dot(a, win_ref[:, _C_KR:_C_END])
    kslab = jnp.where(rope_lanes, zkr[:, :LANES] * cos + zkr[:, LANES:] * sin, 0.0)
    kr_ref[...] = kslab[:, MLA_NOPE:MLA_NOPE + MLA_ROPE]
    kn = _dot(cb, wk_ref[...])
    for h in range(MLA_HEADS):
        sl = slice(h * LANES, (h + 1) * LANES)
        k_ref[:, sl] = (kn[:, sl] + kslab).astype(BF16)

    zr = _dot(a, win_ref[:, _C_R:_C_RV])
    kscale = RET_DK ** -0.5
    for s in range(2):
        sl = slice(s * LANES, (s + 1) * LANES)
        rq_ref[:, sl] = (zr[:, s * LANES:(s + 1) * LANES] * cos
                         + zr[:, 512 + s * LANES:512 + (s + 1) * LANES] * sin).astype(BF16)
        rk_ref[:, sl] = ((zr[:, 256 + s * LANES:256 + (s + 1) * LANES] * cos
                          + zr[:, 768 + s * LANES:768 + (s + 1) * LANES] * sin) * kscale).astype(BF16)
    rv_ref[...] = _dot(a, win_ref[:, _C_RV:_C_RG]).astype(BF16)
    g = _dot(a, win_ref[:, _C_RG:_C_KR])
    rg_ref[...] = (g * jax.nn.sigmoid(g)).astype(BF16)


def _proj(h, tab, norm_w, q_norm_w, kv_norm_w, w, *, tm, n_pos, n_rep):
    N = h.shape[0]
    row = lambda s, b: (b * n_pos + s, 0)
    outs = [(MLA_HEADS * LANES, BF16), (MLA_HEADS * LANES, BF16), (MLA_HEADS * MLA_V, BF16),
            (MLA_KV_LORA, F32), (MLA_ROPE, F32), (256, BF16), (256, BF16), (512, BF16), (512, BF16)]
    return pl.pallas_call(
        _proj_kernel,
        grid=(n_pos, n_rep),
        in_specs=[
            pl.BlockSpec((tm, D_MODEL), row),
            pl.BlockSpec((2, tm, LANES), lambda s, b: (0, s, 0)),
            _const_spec((1, D_MODEL)), _const_spec((D_MODEL, _C_END)),
            _const_spec((1, MLA_Q_LORA)), _const_spec(w['wq'].shape), _const_spec(w['wqr'].shape),
            _const_spec((1, MLA_KV_LORA)), _const_spec(w['wk'].shape), _const_spec(w['wv'].shape),
        ],
        out_specs=[pl.BlockSpec((tm, c), row) for c, _ in outs],
        out_shape=[jax.ShapeDtypeStruct((N, c), dt) for c, dt in outs],
        compiler_params=_params("arbitrary", "arbitrary"),
        name="proj",
    )(h, tab, norm_w.reshape(1, -1), w['win'], q_norm_w.reshape(1, -1), w['wq'], w['wqr'],
      kv_norm_w.reshape(1, -1), w['wk'], w['wv'])


def _expand_kernel(c_ref, kr_ref, wk_ref, place_ref, wv_ref, k_ref, v_ref):
    cb = c_ref[...].astype(BF16)
    k_ref[...] = (_dot(cb, wk_ref[...]) + _dot(kr_ref[...].astype(BF16), place_ref[...])).astype(BF16)
    v_ref[...] = _dot(cb, wv_ref[...]).astype(BF16)


def _expand(ckv, krope, w, *, tm):
    N = ckv.shape[0]
    row = lambda i: (i, 0)
    return pl.pallas_call(
        _expand_kernel,
        grid=(N // tm,),
        in_specs=[pl.BlockSpec((tm, MLA_KV_LORA), row), pl.BlockSpec((tm, MLA_ROPE), row),
                  _const_spec(w['wk'].shape), _const_spec(w['place'].shape), _const_spec(w['wv'].shape)],
        out_specs=[pl.BlockSpec((tm, MLA_HEADS * LANES), row), pl.BlockSpec((tm, MLA_HEADS * MLA_V), row)],
        out_shape=[jax.ShapeDtypeStruct((N, MLA_HEADS * LANES), BF16),
                   jax.ShapeDtypeStruct((N, MLA_HEADS * MLA_V), BF16)],
        compiler_params=_params("arbitrary"),
        name="expand",
    )(ckv, krope, w['wk'], w['place'], w['wv'])


def _softmax_step(carry, s, vt):
    m, l, acc = carry
    m_new = jnp.maximum(m, jnp.max(s, axis=-1, keepdims=True))
    alpha = jnp.exp2(m - m_new)
    p = jnp.exp2(s - m_new)
    l = alpha * l + jnp.sum(p, axis=-1, keepdims=True)
    acc = alpha * acc + _dot(p.astype(BF16), vt)
    return m_new, l, acc


def _attn_prompt_kernel(q_ref, k_ref, v_ref, o_ref, *, t):
    i = pl.program_id(2)
    qs = [q_ref[:, hh * LANES:(hh + 1) * LANES] for hh in range(2)]

    def tile(j, carries, masked):
        start = pl.multiple_of(j * t, t)
        vt = v_ref[pl.ds(start, t), :]
        out = []
        for hh in range(2):
            s = _dot_nt(qs[hh], k_ref[pl.ds(start, t), hh * LANES:(hh + 1) * LANES])
            if masked:
                rc = lax.broadcasted_iota(jnp.int32, (t, t), 0) // CHUNK
                cc = lax.broadcasted_iota(jnp.int32, (t, t), 1) // CHUNK
                s = jnp.where(cc <= rc, s, NEG)
            out.append(_softmax_step(carries[hh], s, vt))
        return tuple(out)

    init = tuple((jnp.full((t, 1), NEG, F32), jnp.zeros((t, 1), F32), jnp.zeros((t, LANES), F32))
                 for _ in range(2))
    carries = lax.fori_loop(0, i, lambda j, c: tile(j, c, False), init)
    (_, l0, a0), (_, l1, a1) = tile(i, carries, True)
    lane = lax.broadcasted_iota(jnp.int32, (t, LANES), 1)
    o_ref[...] = jnp.where(lane < MLA_V, a0 / l0, a1 / l1).astype(BF16)


def _attn_prompt(q, k, v, *, t):
    B, S, _ = q.shape
    return pl.pallas_call(
        functools.partial(_attn_prompt_kernel, t=t),
        grid=(B, HEAD_PAIRS, S // t),
        in_specs=[pl.BlockSpec((None, t, 2 * LANES), lambda b, p, i: (b, i, p)),
                  pl.BlockSpec((None, S, 2 * LANES), lambda b, p, i: (b, 0, p)),
                  pl.BlockSpec((None, S, LANES), lambda b, p, i: (b, 0, p))],
        out_specs=pl.BlockSpec((None, t, LANES), lambda b, p, i: (b, i, p)),
        out_shape=jax.ShapeDtypeStruct((B, S, MLA_HEADS * MLA_V), BF16),
        compiler_params=_params("arbitrary", "arbitrary", "arbitrary"),
        name="attn_prompt",
    )(q, k, v)


def _attn_sample_kernel(q_ref, kc_ref, vc_ref, kn_ref, vn_ref, o_ref):
    L = q_ref.shape[0]
    outs = []
    for hh in range(2):
        sl = slice(hh * LANES, (hh + 1) * LANES)
        qh = q_ref[:, sl]
        sc = _dot_nt(qh, kc_ref[:, sl])
        sn = _dot_nt(qh, kn_ref[:, sl])
        m = jnp.maximum(jnp.max(sc, axis=-1, keepdims=True), jnp.max(sn, axis=-1, keepdims=True))
        pc, pn = jnp.exp2(sc - m), jnp.exp2(sn - m)
        l = jnp.sum(pc, axis=-1, keepdims=True) + jnp.sum(pn, axis=-1, keepdims=True)
        acc = _dot(pc.astype(BF16), vc_ref[...]) + _dot(pn.astype(BF16), vn_ref[...])
        outs.append(acc / l)
    lane = lax.broadcasted_iota(jnp.int32, (L, LANES), 1)
    o_ref[...] = jnp.where(lane < MLA_V, outs[0], outs[1]).astype(BF16)


def _attn_sample(q, kc, vc, kn, vn):
    B, L, _ = q.shape
    P = kc.shape[1]
    return pl.pallas_call(
        _attn_sample_kernel,
        grid=(B, HEAD_PAIRS),
        in_specs=[pl.BlockSpec((None, L, 2 * LANES), lambda b, p: (b, 0, p)),
                  pl.BlockSpec((None, P, 2 * LANES), lambda b, p: (b, 0, p)),
                  pl.BlockSpec((None, P, LANES), lambda b, p: (b, 0, p)),
                  pl.BlockSpec((None, L, 2 * LANES), lambda b, p: (b, 0, p)),
                  pl.BlockSpec((None, L, LANES), lambda b, p: (b, 0, p))],
        out_specs=pl.BlockSpec((None, L, LANES), lambda b, p: (b, 0, p)),
        out_shape=jax.ShapeDtypeStruct((B, L, MLA_HEADS * MLA_V), BF16),
        compiler_params=_params("arbitrary", "arbitrary"),
        name="attn_sample",
    )(q, kc, vc, kn, vn)


def _ret_kernel(rq_ref, rk_ref, rv_ref, g_ref, t0_ref, dec_ref, qd_ref, kd_ref, rs_ref, gnw_ref,
                o_ref, st_ref):
    L = rq_ref.shape[0]

    @pl.when(pl.program_id(1) == 0)
    def _():
        st_ref[...] = t0_ref[...]

    lane = lax.broadcasted_iota(jnp.int32, (L, LANES), 1)
    low = lane < RET_DV
    srow = lax.broadcasted_iota(jnp.int32, (LANES, LANES), 0) // RET_DK
    scol = lax.broadcasted_iota(jnp.int32, (LANES, LANES), 1) // RET_DV
    for p in range(HEAD_PAIRS):
        slab = p // 2
        sl_qk = slice(slab * LANES, (slab + 1) * LANES)
        sl_v = slice(p * LANES, (p + 1) * LANES)
        qs, ks, vp = rq_ref[:, sl_qk], rk_ref[:, sl_qk], rv_ref[:, sl_v]
        state = st_ref[p]
        qdec = (qs.astype(F32) * qd_ref[slab]).astype(BF16)
        o = _dot(qdec, state.astype(BF16))
        for c in range(2):
            h = 2 * p + c
            qm = jnp.where(lane // RET_DK == h % 4, qs, jnp.zeros_like(qs))
            inner = _dot_nt(qm, ks) * dec_ref[h]
            oh = _dot(inner.astype(BF16), vp)
            o = o + jnp.where(low if c == 0 else ~low, oh, 0.0)
        kdec = (ks.astype(F32) * kd_ref[slab]).astype(BF16)
        new_state = rs_ref[slab] * state + _dot_tn(kdec, vp)
        st_ref[p] = jnp.where(srow == 2 * (p % 2) + scol, new_state, 0.0)

        def half_mean(x):
            lo = jnp.sum(jnp.where(low, x, 0.0), axis=-1, keepdims=True)
            hi = jnp.sum(jnp.where(low, 0.0, x), axis=-1, keepdims=True)
            return jnp.where(low, lo, hi) * (1.0 / RET_DV)
        d = o - half_mean(o)
        on = d * lax.rsqrt(half_mean(d * d) + EPS) * gnw_ref[:, sl_v]
        o_ref[:, sl_v] = (g_ref[:, sl_v].astype(F32) * on).astype(BF16)


def _retention(rq, rk, rv, g, t0, gn_w, *, L):
    B, S, _ = rq.shape
    dec, qd, kd, rs = _retention_tables(L)
    blk = lambda b, i: (b, i, 0)
    st = lambda b, i: (b, 0, 0, 0)
    return pl.pallas_call(
        _ret_kernel,
        grid=(B, S // L),
        in_specs=[pl.BlockSpec((None, L, 256), blk), pl.BlockSpec((None, L, 256), blk),
                  pl.BlockSpec((None, L, 512), blk), pl.BlockSpec((None, L, 512), blk),
                  pl.BlockSpec((None, HEAD_PAIRS, LANES, LANES), st),
                  _const_spec(dec.shape), _const_spec(qd.shape), _const_spec(kd.shape),
                  _const_spec(rs.shape), _const_spec((1, 512))],
        out_specs=[pl.BlockSpec((None, L, 512), blk),
                   pl.BlockSpec((None, HEAD_PAIRS, LANES, LANES), st)],
        out_shape=[jax.ShapeDtypeStruct((B, S, 512), BF16),
                   jax.ShapeDtypeStruct((B, HEAD_PAIRS, LANES, LANES), F32)],
        compiler_params=_params("arbitrary", "arbitrary"),
        name="retention",
    )(rq, rk, rv, g, t0, dec, qd, kd, rs, gn_w.reshape(1, -1))


def _post_kernel(h_ref, att_ref, ret_ref, p_ref, woa_ref, wor_ref, nf_ref, w1_ref, w2_ref,
                 np_ref, wg_ref, wp_ref, fin_ref, o_ref, *, final, fc):
    h = h_ref[...] + _dot(att_ref[...], woa_ref[...]) + _dot(ret_ref[...], wor_ref[...])
    a = _rms(h, nf_ref[...]).astype(BF16)
    acc = jnp.zeros_like(h)
    for c in range(D_FF // fc):
        u = _dot(a, w1_ref[:, c * fc:(c + 1) * fc])
        acc = acc + _dot(jnp.square(jnp.maximum(u, 0.0)).astype(BF16), w2_ref[c * fc:(c + 1) * fc, :])
    h = h + acc
    gate = jax.nn.sigmoid(_dot(_rms(h, np_ref[...]).astype(BF16), wg_ref[...]))
    h = h + _dot(p_ref[...].astype(BF16), wp_ref[...]) * gate
    o_ref[...] = _rms(h, fin_ref[...]) if final else h


def _post(h, att, ret, p, norm_ffn_w, norm_ple_w, final_norm_w, w, *, tm, final):
    N = h.shape[0]
    row = lambda i: (i, 0)
    return pl.pallas_call(
        functools.partial(_post_kernel, final=final, fc=512),
        grid=(N // tm,),
        in_specs=[pl.BlockSpec((tm, D_MODEL), row), pl.BlockSpec((tm, 512), row),
                  pl.BlockSpec((tm, 512), row), pl.BlockSpec((tm, PLE_DIM), row),
                  _const_spec(w['woa'].shape), _const_spec(w['wor'].shape), _const_spec((1, D_MODEL)),
                  _const_spec(w['w1'].shape), _const_spec(w['w2'].shape), _const_spec((1, D_MODEL)),
                  _const_spec(w['wg'].shape), _const_spec(w['wp'].shape), _const_spec((1, D_MODEL))],
        out_specs=pl.BlockSpec((tm, D_MODEL), row),
        out_shape=jax.ShapeDtypeStruct((N, D_MODEL), F32),
        compiler_params=_params("arbitrary"),
        name="post",
    )(h, att, ret, p, w['woa'], w['wor'], norm_ffn_w.reshape(1, -1), w['w1'], w['w2'],
      norm_ple_w.reshape(1, -1), w['wg'], w['wp'], final_norm_w.reshape(1, -1))


PROMPT_TM = 512
ATTN_T = 512
RET_BLOCK = 256
EXPAND_TM = 1024


def kernel(x_prompt, x_sample, cache_ckv, cache_krope, state_ret, p_prompt, p_sample, norm_mix_w, w_in,
           q_norm_w, w_uq, kv_norm_w, w_ukv, ret_gn_w, w_out, norm_ffn_w, w_ff1, w_ff2, norm_ple_w,
           w_ple_gate, w_ple_proj, final_norm_w):
    B, S, D = x_prompt.shape
    Bd, Ld, _ = x_sample.shape
    depth, _, past, _ = cache_ckv.shape
    assert past % CHUNK == 0 and Ld <= CHUNK and S % ATTN_T == 0 and ATTN_T % CHUNK == 0
    tm_p = min(PROMPT_TM, S)
    tm_c = min(EXPAND_TM, Bd * past)

    tab_p = _rope_table(jnp.arange(S, dtype=jnp.int32))
    tab_s = jnp.tile(_rope_table(past + jnp.arange(Ld, dtype=jnp.int32)), (1, Bd, 1))
    zero_state = jnp.zeros((B, HEAD_PAIRS, LANES, LANES), F32)

    hp = x_prompt.reshape(B * S, D)
    hs = x_sample.reshape(Bd * Ld, D)
    outs = [[] for _ in range(6)]
    for i in range(depth):
        w = _layer_weights(w_in[i], w_uq[i], w_ukv[i], w_out[i], w_ff1[i], w_ff2[i],
                           w_ple_gate[i], w_ple_proj[i])
        final = i == depth - 1

        q, k, v, ckv, kr, rq, rk, rv, rg = _proj(hp, tab_p, norm_mix_w[i], q_norm_w[i], kv_norm_w[i], w,
                                                 tm=tm_p, n_pos=S // tm_p, n_rep=B)
        att = _attn_prompt(q.reshape(B, S, -1), k.reshape(B, S, -1), v.reshape(B, S, -1), t=ATTN_T)
        ret, st = _retention(rq.reshape(B, S, -1), rk.reshape(B, S, -1), rv.reshape(B, S, -1),
                             rg.reshape(B, S, -1), zero_state, ret_gn_w[i], L=RET_BLOCK)
        hp = _post(hp, att.reshape(B * S, -1), ret.reshape(B * S, -1), p_prompt[i].reshape(B * S, -1),
                   norm_ffn_w[i], norm_ple_w[i], final_norm_w, w, tm=tm_p, final=final)
        outs[0].append(ckv.reshape(B, S, -1))
        outs[1].append(kr.reshape(B, S, -1))
        outs[2].append(_slabs_to_state(st).astype(x_prompt.dtype))

        q, k, v, ckv, kr, rq, rk, rv, rg = _proj(hs, tab_s, norm_mix_w[i], q_norm_w[i], kv_norm_w[i], w,
                                                 tm=Bd * Ld, n_pos=1, n_rep=1)
        kc, vc = _expand(cache_ckv[i].reshape(Bd * past, -1), cache_krope[i].reshape(Bd * past, -1), w,
                         tm=tm_c)
        att = _attn_sample(q.reshape(Bd, Ld, -1), kc.reshape(Bd, past, -1), vc.reshape(Bd, past, -1),
                           k.reshape(Bd, Ld, -1), v.reshape(Bd, Ld, -1))
        ret, st = _retention(rq.reshape(Bd, Ld, -1), rk.reshape(Bd, Ld, -1), rv.reshape(Bd, Ld, -1),
                             rg.reshape(Bd, Ld, -1), _state_to_slabs(state_ret[i]), ret_gn_w[i], L=Ld)
        hs = _post(hs, att.reshape(Bd * Ld, -1), ret.reshape(Bd * Ld, -1),
                   p_sample[i].reshape(Bd * Ld, -1), norm_ffn_w[i], norm_ple_w[i], final_norm_w, w,
                   tm=Bd * Ld, final=final)
        outs[3].append(ckv.reshape(Bd, Ld, -1))
        outs[4].append(kr.reshape(Bd, Ld, -1))
        outs[5].append(_slabs_to_state(st).astype(x_sample.dtype))

    return (hp.reshape(B, S, D), hs.reshape(Bd, Ld, D), jnp.stack(outs[0]), jnp.stack(outs[1]),
            jnp.stack(outs[2]), jnp.stack(outs[3]), jnp.stack(outs[4]), jnp.stack(outs[5]))
```

```python
import functools

import numpy as np
import jax
import jax.numpy as jnp
from jax import lax
from jax.experimental import pallas as pl
from jax.experimental.pallas import tpu as pltpu

D_MODEL = 1024
CHUNK = 64
PLE_DIM = 256
D_FF = 4 * D_MODEL
MLA_HEADS = 8
MLA_NOPE = 64
MLA_ROPE = 32
MLA_V = 64
MLA_Q_LORA = 384
MLA_KV_LORA = 256
MLA_SCALE = (MLA_NOPE + MLA_ROPE) ** -0.5
Q_SCALE = MLA_SCALE * float(np.log2(np.e))
RET_HEADS = 8
RET_DK = 32
RET_DV = 64
ROPE_THETA = 10000.0
EPS = 1e-6
NEG = -1e30

LANES = 128
HEAD_PAIRS = MLA_HEADS // 2
V_ROWS = MLA_V + 16
VMEM_LIMIT = 56 * 1024 * 1024

_C_Q, _C_C, _C_R, _C_RV, _C_RG, _C_KR, _C_END = 0, 384, 640, 1664, 2176, 2688, 2944

F32 = jnp.float32
BF16 = jnp.bfloat16


def _rot_cols(base, nheads, hd):
    half = hd // 2
    cols, sign = [], []
    for h in range(nheads):
        for d in range(hd):
            cols.append(base + h * hd + (d + half if d < half else d - half))
            sign.append(-1.0 if d < half else 1.0)
    return cols, sign


def _in_proj_layout():
    o_q, o_c, o_kr, o_rq, o_rk, o_rv, o_rg = 0, 384, 640, 672, 928, 1184, 1696
    cols, mult = [], []

    def plain(base, n):
        cols.extend(range(base, base + n))
        mult.extend([1.0] * n)

    def slab(c, s):
        cols.extend([0] * 64 + list(c) + [0] * 32)
        mult.extend([0.0] * 64 + list(s) + [0.0] * 32)

    plain(o_q, MLA_Q_LORA)
    plain(o_c, MLA_KV_LORA)
    plain(o_rq, 256)
    plain(o_rk, 256)
    for base in (o_rq, o_rk):
        c, s = _rot_cols(base, RET_HEADS, RET_DK)
        cols.extend(c)
        mult.extend(s)
    plain(o_rv, 512)
    plain(o_rg, 512)
    slab(range(o_kr, o_kr + MLA_ROPE), [1.0] * MLA_ROPE)
    slab(*_rot_cols(o_kr, 1, MLA_ROPE))
    assert len(cols) == _C_END
    return np.asarray(cols, np.int32), np.asarray(mult, np.float32)


def _uq_layout():
    hd = MLA_NOPE + MLA_ROPE
    cols, mult, rcols, rmult = [], [], [], []
    for h in range(MLA_HEADS):
        cols.extend(list(range(h * hd, h * hd + hd)) + [0] * 32)
        mult.extend([1.0] * hd + [0.0] * 32)
        c, s = _rot_cols(h * hd + MLA_NOPE, 1, MLA_ROPE)
        rcols.extend([0] * 64 + c + [0] * 32)
        rmult.extend([0.0] * 64 + s + [0.0] * 32)
    return (np.asarray(cols, np.int32), np.asarray(mult, np.float32),
            np.asarray(rcols, np.int32), np.asarray(rmult, np.float32))


def _ukv_layout():
    hd = MLA_NOPE + MLA_V
    kcols, kmult, vcols = [], [], []
    for h in range(MLA_HEADS):
        kcols.extend(list(range(h * hd, h * hd + MLA_NOPE)) + [0] * 64)
        kmult.extend([1.0] * MLA_NOPE + [0.0] * 64)
        vcols.extend(range(h * hd + MLA_NOPE, (h + 1) * hd))
    return np.asarray(kcols, np.int32), np.asarray(kmult, np.float32), np.asarray(vcols, np.int32)


def _take_cols(w, cols, mult):
    return (jnp.take(w, jnp.asarray(cols), axis=1) * jnp.asarray(mult)[None, :]).astype(BF16)


def _layer_weights(w_in, w_uq, w_ukv, w_out, w_ff1, w_ff2, w_ple_gate, w_ple_proj):
    ic, im = _in_proj_layout()
    qc, qm, qrc, qrm = _uq_layout()
    kc, km, vc = _ukv_layout()
    place = np.zeros((MLA_ROPE, MLA_HEADS * LANES), np.float32)
    for h in range(MLA_HEADS):
        for r in range(MLA_ROPE):
            place[r, h * LANES + MLA_NOPE + r] = 1.0
    return dict(
        win=_take_cols(w_in, ic, im),
        wq=_take_cols(w_uq, qc, qm),
        wqr=_take_cols(w_uq, qrc, qrm),
        wk=_take_cols(w_ukv, kc, km),
        wv=jnp.take(w_ukv, jnp.asarray(vc), axis=1).astype(BF16),
        wvt=jnp.take(w_ukv, jnp.asarray(vc), axis=1).T.astype(BF16),
        place=jnp.asarray(place, BF16),
        woa=w_out[:MLA_HEADS * MLA_V].astype(BF16),
        wor=w_out[MLA_HEADS * MLA_V:].astype(BF16),
        w1=w_ff1.astype(BF16),
        w2=w_ff2.astype(BF16),
        wg=w_ple_gate.astype(BF16),
        wp=w_ple_proj.astype(BF16),
    )


def _rope_table(pos):
    half = MLA_ROPE // 2
    inv = ROPE_THETA ** (-jnp.arange(half, dtype=F32) / half)
    ang = pos.astype(F32)[:, None] * inv[None, :]
    cos, sin = jnp.cos(ang), jnp.sin(ang)
    return jnp.stack([jnp.tile(cos, (1, LANES // half)), jnp.tile(sin, (1, LANES // half))])


def _retention_tables(L):
    log_gamma = jnp.log1p(-jnp.exp2(-5.0 - jnp.arange(RET_HEADS, dtype=F32)))
    idx = jnp.arange(L, dtype=F32)
    diff = idx[:, None] - idx[None, :]
    dec = jnp.where(diff >= 0, jnp.exp(jnp.maximum(diff, 0.0)[None] * log_gamma[:, None, None]), 0.0)
    lg_lane = jnp.repeat(log_gamma, RET_DK).reshape(2, 1, LANES)
    qd = jnp.exp((idx + 1.0)[None, :, None] * lg_lane)
    kd = jnp.exp((L - 1.0 - idx)[None, :, None] * lg_lane)
    rs = jnp.broadcast_to(jnp.exp(L * lg_lane).reshape(2, LANES, 1), (2, LANES, LANES))
    return dec, qd, kd, rs


def _state_to_slabs(state):
    B = state.shape[0]
    t = jnp.zeros((B, HEAD_PAIRS, 4, RET_DK, 2, RET_DV), F32)
    for h in range(RET_HEADS):
        t = t.at[:, h // 2, h % 4, :, h % 2, :].set(state[:, h].astype(F32))
    return t.reshape(B, HEAD_PAIRS, LANES, LANES)


def _slabs_to_state(t):
    B = t.shape[0]
    t = t.reshape(B, HEAD_PAIRS, 4, RET_DK, 2, RET_DV)
    return jnp.stack([t[:, h // 2, h % 4, :, h % 2, :] for h in range(RET_HEADS)], axis=1)


def _rms(x, w):
    return x * lax.rsqrt(jnp.mean(x * x, axis=-1, keepdims=True) + EPS) * w


def _dot(a, b):
    return jnp.dot(a, b, preferred_element_type=F32)


def _dot_nt(a, b):
    return lax.dot_general(a, b, (((1,), (1,)), ((), ())), preferred_element_type=F32)


def _dot_tn(a, b):
    return lax.dot_general(a, b, (((0,), (0,)), ((), ())), preferred_element_type=F32)


def _const_spec(shape):
    zeros = (0,) * len(shape)
    return pl.BlockSpec(shape, lambda *_: zeros, pipeline_mode=pl.Buffered(1))


def _params(*sem):
    return pltpu.CompilerParams(dimension_semantics=sem, vmem_limit_bytes=VMEM_LIMIT)


def _proj_kernel(h_ref, tab_ref, nw_ref, win_ref, qnw_ref, wq_ref, wqr_ref, kvnw_ref, wk_ref, wv_ref,
                 q_ref, k_ref, v_ref, ckv_ref, kr_ref, rq_ref, rk_ref, rv_ref, rg_ref, *, v_rows):
    tm = h_ref.shape[0]
    cos, sin = tab_ref[0], tab_ref[1]
    lane = lax.broadcasted_iota(jnp.int32, (tm, LANES), 1)
    rope_lanes = (lane >= MLA_NOPE) & (lane < MLA_NOPE + MLA_ROPE)
    a = _rms(h_ref[...], nw_ref[...]).astype(BF16)

    ql = _rms(_dot(a, win_ref[:, _C_Q:_C_C]), qnw_ref[...]).astype(BF16)
    qz, qzr = _dot(ql, wq_ref[...]), _dot(ql, wqr_ref[...])
    cq = jnp.where(lane < MLA_NOPE, Q_SCALE, jnp.where(rope_lanes, cos * Q_SCALE, 0.0))
    sq = jnp.where(rope_lanes, sin * Q_SCALE, 0.0)
    for h in range(MLA_HEADS):
        sl = slice(h * LANES, (h + 1) * LANES)
        q_ref[:, sl] = (qz[:, sl] * cq + qzr[:, sl] * sq).astype(BF16)

    c = _rms(_dot(a, win_ref[:, _C_C:_C_R]), kvnw_ref[...])
    ckv_ref[...] = c
    cb = c.astype(BF16)
    if v_rows:
        vt = _dot_nt(wv_ref[...], cb)
        ones_row = jnp.where(lax.broadcasted_iota(jnp.int32, (V_ROWS - MLA_V, tm), 0) == 0, 1.0, 0.0)
        for h in range(MLA_HEADS):
            v_ref[h, :MLA_V, :] = vt[h * MLA_V:(h + 1) * MLA_V, :].astype(BF16)
            v_ref[h, MLA_V:, :] = ones_row.astype(BF16)
    else:
        v_ref[...] = _dot(cb, wv_ref[...]).astype(BF16)
    zkr = _dot(a, win_ref[:, _C_KR:_C_END])
    kslab = jnp.where(rope_lanes, zkr[:, :LANES] * cos + zkr[:, LANES:] * sin, 0.0)
    kr_ref[...] = kslab[:, MLA_NOPE:MLA_NOPE + MLA_ROPE]
    kn = _dot(cb, wk_ref[...])
    for h in range(MLA_HEADS):
        sl = slice(h * LANES, (h + 1) * LANES)
        k_ref[:, sl] = (kn[:, sl] + kslab).astype(BF16)

    zr = _dot(a, win_ref[:, _C_R:_C_RV])
    kscale = RET_DK ** -0.5
    for s in range(2):
        sl = slice(s * LANES, (s + 1) * LANES)
        rq_ref[:, sl] = (zr[:, s * LANES:(s + 1) * LANES] * cos
                         + zr[:, 512 + s * LANES:512 + (s + 1) * LANES] * sin).astype(BF16)
        rk_ref[:, sl] = ((zr[:, 256 + s * LANES:256 + (s + 1) * LANES] * cos
                          + zr[:, 768 + s * LANES:768 + (s + 1) * LANES] * sin) * kscale).astype(BF16)
    rv_ref[...] = _dot(a, win_ref[:, _C_RV:_C_RG]).astype(BF16)
    g = _dot(a, win_ref[:, _C_RG:_C_KR])
    rg_ref[...] = (g * jax.nn.sigmoid(g)).astype(BF16)


def _proj(h, tab, norm_w, q_norm_w, kv_norm_w, w, *, tm, n_pos, n_rep, v_rows):
    N = h.shape[0]
    row = lambda s, b: (b * n_pos + s, 0)
    outs = [(MLA_HEADS * LANES, BF16), (MLA_HEADS * LANES, BF16), (MLA_HEADS * MLA_V, BF16),
            (MLA_KV_LORA, F32), (MLA_ROPE, F32), (256, BF16), (256, BF16), (512, BF16), (512, BF16)]
    out_specs = [pl.BlockSpec((tm, c), row) for c, _ in outs]
    out_shape = [jax.ShapeDtypeStruct((N, c), dt) for c, dt in outs]
    wv = w['wv']
    if v_rows:
        out_specs[2] = pl.BlockSpec((None, MLA_HEADS, V_ROWS, tm), lambda s, b: (b * n_pos + s, 0, 0, 0))
        out_shape[2] = jax.ShapeDtypeStruct((N // tm, MLA_HEADS, V_ROWS, tm), BF16)
        wv = w['wvt']
    return pl.pallas_call(
        functools.partial(_proj_kernel, v_rows=v_rows),
        grid=(n_pos, n_rep),
        in_specs=[
            pl.BlockSpec((tm, D_MODEL), row),
            pl.BlockSpec((2, tm, LANES), lambda s, b: (0, s, 0)),
            _const_spec((1, D_MODEL)), _const_spec((D_MODEL, _C_END)),
            _const_spec((1, MLA_Q_LORA)), _const_spec(w['wq'].shape), _const_spec(w['wqr'].shape),
            _const_spec((1, MLA_KV_LORA)), _const_spec(w['wk'].shape), _const_spec(wv.shape),
        ],
        out_specs=out_specs,
        out_shape=out_shape,
        compiler_params=_params("arbitrary", "arbitrary"),
        name="proj",
    )(h, tab, norm_w.reshape(1, -1), w['win'], q_norm_w.reshape(1, -1), w['wq'], w['wqr'],
      kv_norm_w.reshape(1, -1), w['wk'], wv)


def _expand_kernel(c_ref, kr_ref, wk_ref, place_ref, wv_ref, k_ref, v_ref):
    cb = c_ref[...].astype(BF16)
    k_ref[...] = (_dot(cb, wk_ref[...]) + _dot(kr_ref[...].astype(BF16), place_ref[...])).astype(BF16)
    v_ref[...] = _dot(cb, wv_ref[...]).astype(BF16)


def _expand(ckv, krope, w, *, tm):
    N = ckv.shape[0]
    row = lambda i: (i, 0)
    return pl.pallas_call(
        _expand_kernel,
        grid=(N // tm,),
        in_specs=[pl.BlockSpec((tm, MLA_KV_LORA), row), pl.BlockSpec((tm, MLA_ROPE), row),
                  _const_spec(w['wk'].shape), _const_spec(w['place'].shape), _const_spec(w['wv'].shape)],
        out_specs=[pl.BlockSpec((tm, MLA_HEADS * LANES), row), pl.BlockSpec((tm, MLA_HEADS * MLA_V), row)],
        out_shape=[jax.ShapeDtypeStruct((N, MLA_HEADS * LANES), BF16),
                   jax.ShapeDtypeStruct((N, MLA_HEADS * MLA_V), BF16)],
        compiler_params=_params("arbitrary"),
        name="expand",
    )(ckv, krope, w['wk'], w['place'], w['wv'])


def _attn_prompt_kernel(q_ref, k_ref, vt_ref, o_ref, s_ref, *, t):
    i = pl.program_id(2)
    qs = [q_ref[:, hh * LANES:(hh + 1) * LANES] for hh in range(2)]

    def scores(hh, j, slot, masked):
        start = pl.multiple_of(j * t, t)
        s = _dot_nt(k_ref[pl.ds(start, t), hh * LANES:(hh + 1) * LANES], qs[hh])
        if masked:
            kc = lax.broadcasted_iota(jnp.int32, (t, t), 0) // CHUNK
            qc = lax.broadcasted_iota(jnp.int32, (t, t), 1) // CHUNK
            s = jnp.where(kc <= qc, s, NEG)
        s_ref[slot, hh] = s
        return jnp.max(s, axis=0, keepdims=True)

    def consume(stats, smax, slot, j):
        out = []
        for hh in range(2):
            m, acc = stats[hh]
            m_new = jnp.maximum(m, smax[hh])
            p = jnp.exp2(s_ref[slot, hh] - m_new).astype(BF16)
            out.append((m_new, jnp.exp2(m - m_new) * acc + _dot(vt_ref[j, hh], p)))
        return tuple(out)

    def step(n, stats, smax, slot):
        nxt = tuple(scores(hh, n, 1 - slot, False) for hh in range(2))
        return consume(stats, smax, slot, jnp.where(n == 0, i, n - 1)), nxt

    def two_steps(r, carry):
        stats, smax = step(2 * r, *carry, 0)
        return step(2 * r + 1, stats, smax, 1)

    init = tuple((jnp.full((1, t), NEG, F32), jnp.zeros((V_ROWS, t), F32)) for _ in range(2))
    smax = tuple(scores(hh, i, 0, True) for hh in range(2))
    stats, smax = lax.fori_loop(0, i // 2, two_steps, (init, smax))
    last = jnp.maximum(i - 1, 0)

    def odd_tail(stats, smax):
        stats, smax = step(i - 1, stats, smax, 0)
        return consume(stats, smax, 1, last)

    (_, a0), (_, a1) = lax.cond(i % 2 == 1, odd_tail, lambda st, sm: consume(st, sm, 0, last), stats, smax)
    o = jnp.concatenate([a[:MLA_V] / a[MLA_V:MLA_V + 1] for a in (a0, a1)], axis=0)
    o_ref[...] = o.T.astype(BF16)


def _attn_prompt(q, k, vt, *, t):
    B, S, _ = q.shape
    return pl.pallas_call(
        functools.partial(_attn_prompt_kernel, t=t),
        grid=(B, HEAD_PAIRS, S // t),
        in_specs=[pl.BlockSpec((None, t, 2 * LANES), lambda b, p, i: (b, i, p)),
                  pl.BlockSpec((None, S, 2 * LANES), lambda b, p, i: (b, 0, p)),
                  pl.BlockSpec((None, S // t, 2, V_ROWS, t), lambda b, p, i: (b, 0, p, 0, 0))],
        out_specs=pl.BlockSpec((None, t, LANES), lambda b, p, i: (b, i, p)),
        out_shape=jax.ShapeDtypeStruct((B, S, MLA_HEADS * MLA_V), BF16),
        scratch_shapes=[pltpu.VMEM((2, 2, t, t), F32)],
        compiler_params=_params("arbitrary", "arbitrary", "arbitrary"),
        name="attn_prompt",
    )(q, k, vt)


def _attn_sample_kernel(q_ref, kc_ref, vc_ref, kn_ref, vn_ref, o_ref):
    L = q_ref.shape[0]
    outs = []
    for hh in range(2):
        sl = slice(hh * LANES, (hh + 1) * LANES)
        qh = q_ref[:, sl]
        sc = _dot_nt(qh, kc_ref[:, sl])
        sn = _dot_nt(qh, kn_ref[:, sl])
        m = jnp.maximum(jnp.max(sc, axis=-1, keepdims=True), jnp.max(sn, axis=-1, keepdims=True))
        pc, pn = jnp.exp2(sc - m), jnp.exp2(sn - m)
        l = jnp.sum(pc, axis=-1, keepdims=True) + jnp.sum(pn, axis=-1, keepdims=True)
        acc = _dot(pc.astype(BF16), vc_ref[...]) + _dot(pn.astype(BF16), vn_ref[...])
        outs.append(acc / l)
    lane = lax.broadcasted_iota(jnp.int32, (L, LANES), 1)
    o_ref[...] = jnp.where(lane < MLA_V, outs[0], outs[1]).astype(BF16)


def _attn_sample(q, kc, vc, kn, vn):
    B, L, _ = q.shape
    P = kc.shape[1]
    return pl.pallas_call(
        _attn_sample_kernel,
        grid=(B, HEAD_PAIRS),
        in_specs=[pl.BlockSpec((None, L, 2 * LANES), lambda b, p: (b, 0, p)),
                  pl.BlockSpec((None, P, 2 * LANES), lambda b, p: (b, 0, p)),
                  pl.BlockSpec((None, P, LANES), lambda b, p: (b, 0, p)),
                  pl.BlockSpec((None, L, 2 * LANES), lambda b, p: (b, 0, p)),
                  pl.BlockSpec((None, L, LANES), lambda b, p: (b, 0, p))],
        out_specs=pl.BlockSpec((None, L, LANES), lambda b, p: (b, 0, p)),
        out_shape=jax.ShapeDtypeStruct((B, L, MLA_HEADS * MLA_V), BF16),
        compiler_params=_params("arbitrary", "arbitrary"),
        name="attn_sample",
    )(q, kc, vc, kn, vn)


def _ret_kernel(rq_ref, rk_ref, rv_ref, g_ref, t0_ref, dec_ref, qd_ref, kd_ref, rs_ref, gnw_ref,
                o_ref, st_ref):
    L = rq_ref.shape[0]

    @pl.when(pl.program_id(1) == 0)
    def _():
        st_ref[...] = t0_ref[...]

    lane = lax.broadcasted_iota(jnp.int32, (L, LANES), 1)
    low = lane < RET_DV
    srow = lax.broadcasted_iota(jnp.int32, (LANES, LANES), 0) // RET_DK
    scol = lax.broadcasted_iota(jnp.int32, (LANES, LANES), 1) // RET_DV
    for p in range(HEAD_PAIRS):
        slab = p // 2
        sl_qk = slice(slab * LANES, (slab + 1) * LANES)
        sl_v = slice(p * LANES, (p + 1) * LANES)
        qs, ks, vp = rq_ref[:, sl_qk], rk_ref[:, sl_qk], rv_ref[:, sl_v]
        state = st_ref[p]
        qdec = (qs.astype(F32) * qd_ref[slab]).astype(BF16)
        o = _dot(qdec, state.astype(BF16))
        for c in range(2):
            h = 2 * p + c
            qm = jnp.where(lane // RET_DK == h % 4, qs, jnp.zeros_like(qs))
            inner = _dot_nt(qm, ks) * dec_ref[h]
            oh = _dot(inner.astype(BF16), vp)
            o = o + jnp.where(low if c == 0 else ~low, oh, 0.0)
        kdec = (ks.astype(F32) * kd_ref[slab]).astype(BF16)
        new_state = rs_ref[slab] * state + _dot_tn(kdec, vp)
        st_ref[p] = jnp.where(srow == 2 * (p % 2) + scol, new_state, 0.0)

        def half_mean(x):
            lo = jnp.sum(jnp.where(low, x, 0.0), axis=-1, keepdims=True)
            hi = jnp.sum(jnp.where(low, 0.0, x), axis=-1, keepdims=True)
            return jnp.where(low, lo, hi) * (1.0 / RET_DV)
        d = o - half_mean(o)
        on = d * lax.rsqrt(half_mean(d * d) + EPS) * gnw_ref[:, sl_v]
        o_ref[:, sl_v] = (g_ref[:, sl_v].astype(F32) * on).astype(BF16)


def _retention(rq, rk, rv, g, t0, gn_w, *, L):
    B, S, _ = rq.shape
    dec, qd, kd, rs = _retention_tables(L)
    blk = lambda b, i: (b, i, 0)
    st = lambda b, i: (b, 0, 0, 0)
    return pl.pallas_call(
        _ret_kernel,
        grid=(B, S // L),
        in_specs=[pl.BlockSpec((None, L, 256), blk), pl.BlockSpec((None, L, 256), blk),
                  pl.BlockSpec((None, L, 512), blk), pl.BlockSpec((None, L, 512), blk),
                  pl.BlockSpec((None, HEAD_PAIRS, LANES, LANES), st),
                  _const_spec(dec.shape), _const_spec(qd.shape), _const_spec(kd.shape),
                  _const_spec(rs.shape), _const_spec((1, 512))],
        out_specs=[pl.BlockSpec((None, L, 512), blk),
                   pl.BlockSpec((None, HEAD_PAIRS, LANES, LANES), st)],
        out_shape=[jax.ShapeDtypeStruct((B, S, 512), BF16),
                   jax.ShapeDtypeStruct((B, HEAD_PAIRS, LANES, LANES), F32)],
        compiler_params=_params("arbitrary", "arbitrary"),
        name="retention",
    )(rq, rk, rv, g, t0, dec, qd, kd, rs, gn_w.reshape(1, -1))


def _post_kernel(h_ref, att_ref, ret_ref, p_ref, woa_ref, wor_ref, nf_ref, w1_ref, w2_ref,
                 np_ref, wg_ref, wp_ref, fin_ref, o_ref, *, final, fc):
    h = h_ref[...] + _dot(att_ref[...], woa_ref[...]) + _dot(ret_ref[...], wor_ref[...])
    a = _rms(h, nf_ref[...]).astype(BF16)
    acc = jnp.zeros_like(h)
    for c in range(D_FF // fc):
        u = _dot(a, w1_ref[:, c * fc:(c + 1) * fc])
        acc = acc + _dot(jnp.square(jnp.maximum(u, 0.0)).astype(BF16), w2_ref[c * fc:(c + 1) * fc, :])
    h = h + acc
    gate = jax.nn.sigmoid(_dot(_rms(h, np_ref[...]).astype(BF16), wg_ref[...]))
    h = h + _dot(p_ref[...].astype(BF16), wp_ref[...]) * gate
    o_ref[...] = _rms(h, fin_ref[...]) if final else h


def _post(h, att, ret, p, norm_ffn_w, norm_ple_w, final_norm_w, w, *, tm, final):
    N = h.shape[0]
    row = lambda i: (i, 0)
    return pl.pallas_call(
        functools.partial(_post_kernel, final=final, fc=512),
        grid=(N // tm,),
        in_specs=[pl.BlockSpec((tm, D_MODEL), row), pl.BlockSpec((tm, 512), row),
                  pl.BlockSpec((tm, 512), row), pl.BlockSpec((tm, PLE_DIM), row),
                  _const_spec(w['woa'].shape), _const_spec(w['wor'].shape), _const_spec((1, D_MODEL)),
                  _const_spec(w['w1'].shape), _const_spec(w['w2'].shape), _const_spec((1, D_MODEL)),
                  _const_spec(w['wg'].shape), _const_spec(w['wp'].shape), _const_spec((1, D_MODEL))],
        out_specs=pl.BlockSpec((tm, D_MODEL), row),
        out_shape=jax.ShapeDtypeStruct((N, D_MODEL), F32),
        compiler_params=_params("arbitrary"),
        name="post",
    )(h, att, ret, p, w['woa'], w['wor'], norm_ffn_w.reshape(1, -1), w['w1'], w['w2'],
      norm_ple_w.reshape(1, -1), w['wg'], w['wp'], final_norm_w.reshape(1, -1))


PROMPT_TM = 512
RET_BLOCK = 256
EXPAND_TM = 1024


def kernel(x_prompt, x_sample, cache_ckv, cache_krope, state_ret, p_prompt, p_sample, norm_mix_w, w_in,
           q_norm_w, w_uq, kv_norm_w, w_ukv, ret_gn_w, w_out, norm_ffn_w, w_ff1, w_ff2, norm_ple_w,
           w_ple_gate, w_ple_proj, final_norm_w):
    B, S, D = x_prompt.shape
    Bd, Ld, _ = x_sample.shape
    depth, _, past, _ = cache_ckv.shape
    tm_p = min(PROMPT_TM, S)
    assert past % CHUNK == 0 and Ld <= CHUNK and S % tm_p == 0 and tm_p % CHUNK == 0
    tm_c = min(EXPAND_TM, Bd * past)

    tab_p = _rope_table(jnp.arange(S, dtype=jnp.int32))
    tab_s = jnp.tile(_rope_table(past + jnp.arange(Ld, dtype=jnp.int32)), (1, Bd, 1))
    zero_state = jnp.zeros((B, HEAD_PAIRS, LANES, LANES), F32)

    hp = x_prompt.reshape(B * S, D)
    hs = x_sample.reshape(Bd * Ld, D)
    outs = [[] for _ in range(6)]
    for i in range(depth):
        w = _layer_weights(w_in[i], w_uq[i], w_ukv[i], w_out[i], w_ff1[i], w_ff2[i],
                           w_ple_gate[i], w_ple_proj[i])
        final = i == depth - 1

        q, k, v, ckv, kr, rq, rk, rv, rg = _proj(hp, tab_p, norm_mix_w[i], q_norm_w[i], kv_norm_w[i], w,
                                                 tm=tm_p, n_pos=S // tm_p, n_rep=B, v_rows=True)
        att = _attn_prompt(q.reshape(B, S, -1), k.reshape(B, S, -1),
                           v.reshape(B, S // tm_p, MLA_HEADS, V_ROWS, tm_p), t=tm_p)
        ret, st = _retention(rq.reshape(B, S, -1), rk.reshape(B, S, -1), rv.reshape(B, S, -1),
                             rg.reshape(B, S, -1), zero_state, ret_gn_w[i], L=RET_BLOCK)
        hp = _post(hp, att.reshape(B * S, -1), ret.reshape(B * S, -1), p_prompt[i].reshape(B * S, -1),
                   norm_ffn_w[i], norm_ple_w[i], final_norm_w, w, tm=tm_p, final=final)
        outs[0].append(ckv.reshape(B, S, -1))
        outs[1].append(kr.reshape(B, S, -1))
        outs[2].append(_slabs_to_state(st).astype(x_prompt.dtype))

        q, k, v, ckv, kr, rq, rk, rv, rg = _proj(hs, tab_s, norm_mix_w[i], q_norm_w[i], kv_norm_w[i], w,
                                                 tm=Bd * Ld, n_pos=1, n_rep=1, v_rows=False)
        kc, vc = _expand(cache_ckv[i].reshape(Bd * past, -1), cache_krope[i].reshape(Bd * past, -1), w,
                         tm=tm_c)
        att = _attn_sample(q.reshape(Bd, Ld, -1), kc.reshape(Bd, past, -1), vc.reshape(Bd, past, -1),
                           k.reshape(Bd, Ld, -1), v.reshape(Bd, Ld, -1))
        ret, st = _retention(rq.reshape(Bd, Ld, -1), rk.reshape(Bd, Ld, -1), rv.reshape(Bd, Ld, -1),
                             rg.reshape(Bd, Ld, -1), _state_to_slabs(state_ret[i]), ret_gn_w[i], L=Ld)
        hs = _post(hs, att.reshape(Bd * Ld, -1), ret.reshape(Bd * Ld, -1),
                   p_sample[i].reshape(Bd * Ld, -1), norm_ffn_w[i], norm_ple_w[i], final_norm_w, w,
                   tm=Bd * Ld, final=final)
        outs[3].append(ckv.reshape(Bd, Ld, -1))
        outs[4].append(kr.reshape(Bd, Ld, -1))
        outs[5].append(_slabs_to_state(st).astype(x_sample.dtype))

    return (hp.reshape(B, S, D), hs.reshape(Bd, Ld, D), jnp.stack(outs[0]), jnp.stack(outs[1]),
            jnp.stack(outs[2]), jnp.stack(outs[3]), jnp.stack(outs[4]), jnp.stack(outs[5]))
```

```python
import functools

import numpy as np
import jax
import jax.numpy as jnp
from jax import lax
from jax.experimental import pallas as pl
from jax.experimental.pallas import tpu as pltpu

D_MODEL = 1024
CHUNK = 64
PLE_DIM = 256
D_FF = 4 * D_MODEL
MLA_HEADS = 8
MLA_NOPE = 64
MLA_ROPE = 32
MLA_V = 64
MLA_Q_LORA = 384
MLA_KV_LORA = 256
MLA_SCALE = (MLA_NOPE + MLA_ROPE) ** -0.5
Q_SCALE = MLA_SCALE * float(np.log2(np.e))
RET_HEADS = 8
RET_DK = 32
RET_DV = 64
ROPE_THETA = 10000.0
EPS = 1e-6
NEG = -1e30
IN_SPLITS = (MLA_Q_LORA, MLA_KV_LORA, MLA_ROPE, RET_HEADS * RET_DK, RET_HEADS * RET_DK,
             RET_HEADS * RET_DV, RET_HEADS * RET_DV)

LANES = 128
HEAD_PAIRS = MLA_HEADS // 2
V_ROWS = MLA_V + 16
VMEM_LIMIT = 56 * 1024 * 1024

_C_Q, _C_C, _C_R, _C_RV, _C_RG, _C_KR, _C_END = 0, 384, 640, 1664, 2176, 2688, 2944

F32 = jnp.float32
BF16 = jnp.bfloat16


def _rotate_half_cols(w, hd):
    half = hd // 2
    wh = w.reshape(*w.shape[:-1], -1, 2, half)
    return jnp.concatenate([-wh[..., 1, :], wh[..., 0, :]], axis=-1).reshape(w.shape)


def _pad_cols(w, left, width):
    lead, n = w.shape[:-1], w.shape[-1]
    return jnp.concatenate([jnp.zeros((*lead, left), w.dtype), w,
                            jnp.zeros((*lead, width - left - n), w.dtype)], axis=-1)


def _prep_weights(w_in, w_uq, w_ukv, w_out, w_ff1, w_ff2, w_ple_gate, w_ple_proj):
    depth = w_in.shape[0]
    offs = np.cumsum((0,) + IN_SPLITS)
    q_lat, ckv, kr, rq, rk, rv, rg = [w_in[..., offs[n]:offs[n + 1]] for n in range(len(IN_SPLITS))]
    win = jnp.concatenate([q_lat, ckv, rq, rk, _rotate_half_cols(rq, RET_DK), _rotate_half_cols(rk, RET_DK),
                           rv, rg, _pad_cols(kr, MLA_NOPE, LANES),
                           _pad_cols(_rotate_half_cols(kr, MLA_ROPE), MLA_NOPE, LANES)], axis=-1)
    assert win.shape[-1] == _C_END
    uq = w_uq.reshape(depth, MLA_Q_LORA, MLA_HEADS, MLA_NOPE + MLA_ROPE)
    wq = _pad_cols(uq, 0, LANES).reshape(depth, MLA_Q_LORA, MLA_HEADS * LANES)
    wqr = _pad_cols(_rotate_half_cols(uq[..., MLA_NOPE:], MLA_ROPE), MLA_NOPE, LANES)
    ukv = w_ukv.reshape(depth, MLA_KV_LORA, MLA_HEADS, MLA_NOPE + MLA_V)
    wv = ukv[..., MLA_NOPE:].reshape(depth, MLA_KV_LORA, MLA_HEADS * MLA_V)
    place = np.zeros((MLA_ROPE, MLA_HEADS * LANES), np.float32)
    for h in range(MLA_HEADS):
        for r in range(MLA_ROPE):
            place[r, h * LANES + MLA_NOPE + r] = 1.0
    return dict(
        win=win.astype(BF16),
        wq=wq.astype(BF16),
        wqr=wqr.reshape(depth, MLA_Q_LORA, MLA_HEADS * LANES).astype(BF16),
        wk=_pad_cols(ukv[..., :MLA_NOPE], 0, LANES).reshape(depth, MLA_KV_LORA, MLA_HEADS * LANES).astype(BF16),
        wv=wv.astype(BF16),
        wvt=jnp.swapaxes(wv, 1, 2).astype(BF16),
        place=jnp.asarray(place, BF16),
        woa=w_out[:, :MLA_HEADS * MLA_V].astype(BF16),
        wor=w_out[:, MLA_HEADS * MLA_V:].astype(BF16),
        w1=w_ff1.astype(BF16),
        w2=w_ff2.astype(BF16),
        wg=w_ple_gate.astype(BF16),
        wp=w_ple_proj.astype(BF16),
    )


def _rope_table(pos):
    half = MLA_ROPE // 2
    inv = ROPE_THETA ** (-jnp.arange(half, dtype=F32) / half)
    ang = pos.astype(F32)[:, None] * inv[None, :]
    cos, sin = jnp.cos(ang), jnp.sin(ang)
    return jnp.stack([jnp.tile(cos, (1, LANES // half)), jnp.tile(sin, (1, LANES // half))])


def _retention_tables(L):
    log_gamma = np.log1p(-np.exp2(-5.0 - np.arange(RET_HEADS)))
    idx = np.arange(L, dtype=np.float64)
    diff = idx[:, None] - idx[None, :]
    dec = np.where(diff >= 0, np.exp(np.maximum(diff, 0.0)[None] * log_gamma[:, None, None]), 0.0)
    lg_lane = np.repeat(log_gamma, RET_DK).reshape(2, 1, LANES)
    qd = np.exp((idx + 1.0)[None, :, None] * lg_lane)
    kd = np.exp((L - 1.0 - idx)[None, :, None] * lg_lane)
    rs = np.broadcast_to(np.exp(L * lg_lane).reshape(2, LANES, 1), (2, LANES, LANES))
    return [jnp.asarray(x, F32) for x in (dec, qd, kd, rs)]


def _state_to_slabs(state):
    B = state.shape[0]
    t = jnp.zeros((B, HEAD_PAIRS, 4, RET_DK, 2, RET_DV), F32)
    for h in range(RET_HEADS):
        t = t.at[:, h // 2, h % 4, :, h % 2, :].set(state[:, h].astype(F32))
    return t.reshape(B, HEAD_PAIRS, LANES, LANES)


def _slabs_to_state(t):
    B = t.shape[0]
    t = t.reshape(B, HEAD_PAIRS, 4, RET_DK, 2, RET_DV)
    return jnp.stack([t[:, h // 2, h % 4, :, h % 2, :] for h in range(RET_HEADS)], axis=1)


def _rms(x, w):
    return x * lax.rsqrt(jnp.mean(x * x, axis=-1, keepdims=True) + EPS) * w


def _dot(a, b):
    return jnp.dot(a, b, preferred_element_type=F32)


def _dot_nt(a, b):
    return lax.dot_general(a, b, (((1,), (1,)), ((), ())), preferred_element_type=F32)


def _dot_tn(a, b):
    return lax.dot_general(a, b, (((0,), (0,)), ((), ())), preferred_element_type=F32)


def _const_spec(shape):
    zeros = (0,) * len(shape)
    return pl.BlockSpec(shape, lambda *_: zeros, pipeline_mode=pl.Buffered(1))


def _layer_spec(stacked, layer):
    zeros = (0,) * (stacked.ndim - 1)
    return pl.BlockSpec((None,) + stacked.shape[1:], lambda *_: (layer,) + zeros,
                        pipeline_mode=pl.Buffered(1))


def _rows(w):
    return w.reshape(w.shape[0], 1, -1)


def _params(*sem):
    return pltpu.CompilerParams(dimension_semantics=sem, vmem_limit_bytes=VMEM_LIMIT)


def _proj_kernel(h_ref, tab_ref, nw_ref, win_ref, qnw_ref, wq_ref, wqr_ref, kvnw_ref, wk_ref, wv_ref,
                 *refs, v_rows, n_prev):
    if n_prev:
        ckv_prev_ref, kr_prev_ref, *refs = refs
    q_ref, k_ref, v_ref, ckv_ref, kr_ref, rq_ref, rk_ref, rv_ref, rg_ref = refs
    tm = h_ref.shape[0]
    cos, sin = tab_ref[0], tab_ref[1]
    lane = lax.broadcasted_iota(jnp.int32, (tm, LANES), 1)
    rope_lanes = (lane >= MLA_NOPE) & (lane < MLA_NOPE + MLA_ROPE)
    a = _rms(h_ref[...], nw_ref[...]).astype(BF16)

    ql = _rms(_dot(a, win_ref[:, _C_Q:_C_C]), qnw_ref[...]).astype(BF16)
    qz, qzr = _dot(ql, wq_ref[...]), _dot(ql, wqr_ref[...])
    cq = jnp.where(lane < MLA_NOPE, Q_SCALE, jnp.where(rope_lanes, cos * Q_SCALE, 0.0))
    sq = jnp.where(rope_lanes, sin * Q_SCALE, 0.0)
    for h in range(MLA_HEADS):
        sl = slice(h * LANES, (h + 1) * LANES)
        q_ref[:, sl] = (qz[:, sl] * cq + qzr[:, sl] * sq).astype(BF16)

    c = _rms(_dot(a, win_ref[:, _C_C:_C_R]), kvnw_ref[...])
    if n_prev:
        ckv_ref[:n_prev] = ckv_prev_ref[...]
        kr_ref[:n_prev] = kr_prev_ref[...]
    ckv_ref[n_prev] = c
    cb = c.astype(BF16)
    if v_rows:
        vt = _dot_nt(wv_ref[...], cb)
        ones_row = jnp.where(lax.broadcasted_iota(jnp.int32, (V_ROWS - MLA_V, tm), 0) == 0, 1.0, 0.0)
        for h in range(MLA_HEADS):
            v_ref[h, :MLA_V, :] = vt[h * MLA_V:(h + 1) * MLA_V, :].astype(BF16)
            v_ref[h, MLA_V:, :] = ones_row.astype(BF16)
    else:
        v_ref[...] = _dot(cb, wv_ref[...]).astype(BF16)
    zkr = _dot(a, win_ref[:, _C_KR:_C_END])
    kslab = jnp.where(rope_lanes, zkr[:, :LANES] * cos + zkr[:, LANES:] * sin, 0.0)
    kr_ref[n_prev] = kslab[:, MLA_NOPE:MLA_NOPE + MLA_ROPE]
    kn = _dot(cb, wk_ref[...])
    for h in range(MLA_HEADS):
        sl = slice(h * LANES, (h + 1) * LANES)
        k_ref[:, sl] = (kn[:, sl] + kslab).astype(BF16)

    zr = _dot(a, win_ref[:, _C_R:_C_RV])
    kscale = RET_DK ** -0.5
    for s in range(2):
        sl = slice(s * LANES, (s + 1) * LANES)
        rq_ref[:, sl] = (zr[:, s * LANES:(s + 1) * LANES] * cos
                         + zr[:, 512 + s * LANES:512 + (s + 1) * LANES] * sin).astype(BF16)
        rk_ref[:, sl] = ((zr[:, 256 + s * LANES:256 + (s + 1) * LANES] * cos
                          + zr[:, 768 + s * LANES:768 + (s + 1) * LANES] * sin) * kscale).astype(BF16)
    rv_ref[...] = _dot(a, win_ref[:, _C_RV:_C_RG]).astype(BF16)
    g = _dot(a, win_ref[:, _C_RG:_C_KR])
    rg_ref[...] = (g * jax.nn.sigmoid(g)).astype(BF16)


def _proj(h, tab, norm_w, q_norm_w, kv_norm_w, w, layer, prev, *, tm, n_pos, n_rep, v_rows):
    N = h.shape[0]
    row = lambda s, b: (b * n_pos + s, 0)
    stacked = lambda s, b: (0, b * n_pos + s, 0)
    outs = [(MLA_HEADS * LANES, BF16), (MLA_HEADS * LANES, BF16), (MLA_HEADS * MLA_V, BF16),
            (MLA_KV_LORA, F32), (MLA_ROPE, F32), (256, BF16), (256, BF16), (512, BF16), (512, BF16)]
    out_specs = [pl.BlockSpec((tm, c), row) for c, _ in outs]
    out_shape = [jax.ShapeDtypeStruct((N, c), dt) for c, dt in outs]
    for n in (3, 4):
        out_specs[n] = pl.BlockSpec((layer + 1, tm, outs[n][0]), stacked)
        out_shape[n] = jax.ShapeDtypeStruct((layer + 1, N, outs[n][0]), F32)
    wv = w['wv']
    if v_rows:
        out_specs[2] = pl.BlockSpec((None, MLA_HEADS, V_ROWS, tm), lambda s, b: (b * n_pos + s, 0, 0, 0))
        out_shape[2] = jax.ShapeDtypeStruct((N // tm, MLA_HEADS, V_ROWS, tm), BF16)
        wv = w['wvt']
    prev = () if prev is None else prev
    assert len(prev) == (2 if layer else 0)
    return pl.pallas_call(
        functools.partial(_proj_kernel, v_rows=v_rows, n_prev=layer),
        grid=(n_pos, n_rep),
        in_specs=[
            pl.BlockSpec((tm, D_MODEL), row),
            pl.BlockSpec((2, tm, LANES), lambda s, b: (0, s, 0)),
            _layer_spec(norm_w, layer), _layer_spec(w['win'], layer),
            _layer_spec(q_norm_w, layer), _layer_spec(w['wq'], layer), _layer_spec(w['wqr'], layer),
            _layer_spec(kv_norm_w, layer), _layer_spec(w['wk'], layer), _layer_spec(wv, layer),
        ] + [pl.BlockSpec((layer, tm, x.shape[-1]), stacked) for x in prev],
        out_specs=out_specs,
        out_shape=out_shape,
        compiler_params=_params("arbitrary", "arbitrary"),
        name="proj",
    )(h, tab, norm_w, w['win'], q_norm_w, w['wq'], w['wqr'], kv_norm_w, w['wk'], wv, *prev)


def _expand_kernel(c_ref, kr_ref, wk_ref, place_ref, wv_ref, k_ref, v_ref):
    cb = c_ref[...].astype(BF16)
    k_ref[...] = (_dot(cb, wk_ref[...]) + _dot(kr_ref[...].astype(BF16), place_ref[...])).astype(BF16)
    v_ref[...] = _dot(cb, wv_ref[...]).astype(BF16)


def _expand(ckv, krope, w, layer, *, tm):
    N = ckv.shape[0] // w['wk'].shape[0]
    row = lambda i: (i, 0)
    cached = lambda i: (layer * (N // tm) + i, 0)
    return pl.pallas_call(
        _expand_kernel,
        grid=(N // tm,),
        in_specs=[pl.BlockSpec((tm, MLA_KV_LORA), cached), pl.BlockSpec((tm, MLA_ROPE), cached),
                  _layer_spec(w['wk'], layer), _const_spec(w['place'].shape), _layer_spec(w['wv'], layer)],
        out_specs=[pl.BlockSpec((tm, MLA_HEADS * LANES), row), pl.BlockSpec((tm, MLA_HEADS * MLA_V), row)],
        out_shape=[jax.ShapeDtypeStruct((N, MLA_HEADS * LANES), BF16),
                   jax.ShapeDtypeStruct((N, MLA_HEADS * MLA_V), BF16)],
        compiler_params=_params("arbitrary"),
        name="expand",
    )(ckv, krope, w['wk'], w['place'], w['wv'])


def _visible_cap(t):
    kc = np.arange(t)[:, None] // CHUNK
    qc = np.arange(t)[None, :] // CHUNK
    return jnp.asarray(np.where(kc <= qc, np.finfo(np.float32).max, NEG), F32)


def _attn_prompt_kernel(q_ref, k_ref, vt_ref, cap_ref, o_ref, s_ref, acc_ref, m_ref, *, t, n_tiles):
    acc_ref[...] = jnp.zeros(acc_ref.shape, F32)
    m_ref[...] = jnp.full(m_ref.shape, NEG, F32)

    def produce(item, slot, masked):
        qi, kj = item
        q0 = pl.multiple_of(qi * t, t)
        k0 = pl.multiple_of(kj * t, t)
        out = []
        for hh in range(2):
            sl = slice(hh * LANES, (hh + 1) * LANES)
            s = _dot_nt(k_ref[pl.ds(k0, t), sl], q_ref[pl.ds(q0, t), sl])
            if masked:
                cols = []
                for lt in range(t // LANES):
                    r0 = (lt * LANES // CHUNK + 1) * CHUNK
                    col = s[:, lt * LANES:(lt + 1) * LANES]
                    capped = jnp.minimum(col[r0:], cap_ref[r0:, lt * LANES:(lt + 1) * LANES])
                    cols.append(jnp.concatenate([col[:r0], capped], axis=0))
                s = jnp.concatenate(cols, axis=1)
            s_ref[slot, hh] = s
            out.append(jnp.max(s, axis=0, keepdims=True))
        return tuple(out)

    def consume(smax, slot, item):
        qi, kj = item
        for hh in range(2):
            m = m_ref[qi, hh]
            m_new = jnp.maximum(m, smax[hh])
            p = jnp.exp2(s_ref[slot, hh] - m_new).astype(BF16)
            acc_ref[qi, hh] = jnp.exp2(m - m_new) * acc_ref[qi, hh] + _dot(vt_ref[kj, hh], p)
            m_ref[qi, hh] = m_new

    def run(first, count, advance, masked):
        if count == 0:
            return

        def two_steps(r, carry):
            smax, prev, cur = carry
            nxt = produce(cur, 1, masked)
            consume(smax, 0, prev)
            prev, cur = cur, advance(cur)
            smax = produce(cur, 0, masked)
            consume(nxt, 1, prev)
            return smax, cur, advance(cur)

        smax, prev, cur = lax.fori_loop(0, (count - 1) // 2, two_steps,
                                        (produce(first, 0, masked), first, advance(first)))
        if count % 2 == 0:
            nxt = produce(cur, 1, masked)
            consume(smax, 0, prev)
            consume(nxt, 1, cur)
        else:
            consume(smax, 0, prev)

    def next_below(item):
        qi, kj = item
        last = kj == qi - 1
        return jnp.where(last, qi + 1, qi), jnp.where(last, 0, kj + 1)

    run((jnp.int32(1), jnp.int32(0)), n_tiles * (n_tiles - 1) // 2, next_below, False)
    run((jnp.int32(0), jnp.int32(0)), n_tiles, lambda item: (item[0] + 1, item[1] + 1), True)

    def finish(qi, carry):
        heads = [acc_ref[qi, hh, :MLA_V] * (1.0 / acc_ref[qi, hh, MLA_V:MLA_V + 1]) for hh in range(2)]
        o_ref[pl.ds(pl.multiple_of(qi * t, t), t), :] = jnp.concatenate(heads, axis=0).T.astype(BF16)
        return carry

    lax.fori_loop(0, n_tiles, finish, 0)


def _attn_prompt(q, k, vt, *, t):
    B, S, _ = q.shape
    n_tiles = S // t
    return pl.pallas_call(
        functools.partial(_attn_prompt_kernel, t=t, n_tiles=n_tiles),
        grid=(B, HEAD_PAIRS),
        in_specs=[pl.BlockSpec((None, S, 2 * LANES), lambda b, p: (b, 0, p)),
                  pl.BlockSpec((None, S, 2 * LANES), lambda b, p: (b, 0, p)),
                  pl.BlockSpec((None, n_tiles, 2, V_ROWS, t), lambda b, p: (b, 0, p, 0, 0)),
                  _const_spec((t, t))],
        out_specs=pl.BlockSpec((None, S, LANES), lambda b, p: (b, 0, p)),
        out_shape=jax.ShapeDtypeStruct((B, S, MLA_HEADS * MLA_V), BF16),
        scratch_shapes=[pltpu.VMEM((2, 2, t, t), F32), pltpu.VMEM((n_tiles, 2, V_ROWS, t), F32),
                        pltpu.VMEM((n_tiles, 2, 1, t), F32)],
        compiler_params=_params("arbitrary", "arbitrary"),
        name="attn_prompt",
    )(q, k, vt, _visible_cap(t))


def _attn_sample_kernel(q_ref, kc_ref, vc_ref, kn_ref, vn_ref, o_ref):
    L = q_ref.shape[0]
    outs = []
    for hh in range(2):
        sl = slice(hh * LANES, (hh + 1) * LANES)
        qh = q_ref[:, sl]
        sc = _dot_nt(qh, kc_ref[:, sl])
        sn = _dot_nt(qh, kn_ref[:, sl])
        m = jnp.maximum(jnp.max(sc, axis=-1, keepdims=True), jnp.max(sn, axis=-1, keepdims=True))
        pc, pn = jnp.exp2(sc - m), jnp.exp2(sn - m)
        l = jnp.sum(pc, axis=-1, keepdims=True) + jnp.sum(pn, axis=-1, keepdims=True)
        acc = _dot(pc.astype(BF16), vc_ref[...]) + _dot(pn.astype(BF16), vn_ref[...])
        outs.append(acc / l)
    lane = lax.broadcasted_iota(jnp.int32, (L, LANES), 1)
    o_ref[...] = jnp.where(lane < MLA_V, outs[0], outs[1]).astype(BF16)


def _attn_sample(q, kc, vc, kn, vn):
    B, L, _ = q.shape
    P = kc.shape[1]
    return pl.pallas_call(
        _attn_sample_kernel,
        grid=(B, HEAD_PAIRS),
        in_specs=[pl.BlockSpec((None, L, 2 * LANES), lambda b, p: (b, 0, p)),
                  pl.BlockSpec((None, P, 2 * LANES), lambda b, p: (b, 0, p)),
                  pl.BlockSpec((None, P, LANES), lambda b, p: (b, 0, p)),
                  pl.BlockSpec((None, L, 2 * LANES), lambda b, p: (b, 0, p)),
                  pl.BlockSpec((None, L, LANES), lambda b, p: (b, 0, p))],
        out_specs=pl.BlockSpec((None, L, LANES), lambda b, p: (b, 0, p)),
        out_shape=jax.ShapeDtypeStruct((B, L, MLA_HEADS * MLA_V), BF16),
        compiler_params=_params("arbitrary", "arbitrary"),
        name="attn_sample",
    )(q, kc, vc, kn, vn)


def _ret_kernel(rq_ref, rk_ref, rv_ref, g_ref, t0_ref, dec_ref, qd_ref, kd_ref, rs_ref, gnw_ref,
                o_ref, st_ref):
    L = rq_ref.shape[0]

    @pl.when(pl.program_id(1) == 0)
    def _():
        st_ref[...] = t0_ref[...]

    lane = lax.broadcasted_iota(jnp.int32, (L, LANES), 1)
    low = lane < RET_DV
    srow = lax.broadcasted_iota(jnp.int32, (LANES, LANES), 0) // RET_DK
    scol = lax.broadcasted_iota(jnp.int32, (LANES, LANES), 1) // RET_DV
    for p in range(HEAD_PAIRS):
        slab = p // 2
        sl_qk = slice(slab * LANES, (slab + 1) * LANES)
        sl_v = slice(p * LANES, (p + 1) * LANES)
        qs, ks, vp = rq_ref[:, sl_qk], rk_ref[:, sl_qk], rv_ref[:, sl_v]
        state = st_ref[p]
        qdec = (qs.astype(F32) * qd_ref[slab]).astype(BF16)
        o = _dot(qdec, state.astype(BF16))
        for c in range(2):
            h = 2 * p + c
            qm = jnp.where(lane // RET_DK == h % 4, qs, jnp.zeros_like(qs))
            inner = _dot_nt(qm, ks) * dec_ref[h]
            oh = _dot(inner.astype(BF16), vp)
            o = o + jnp.where(low if c == 0 else ~low, oh, 0.0)
        kdec = (ks.astype(F32) * kd_ref[slab]).astype(BF16)
        new_state = rs_ref[slab] * state + _dot_tn(kdec, vp)
        st_ref[p] = jnp.where(srow == 2 * (p % 2) + scol, new_state, 0.0)

        def half_mean(x):
            lo = jnp.sum(jnp.where(low, x, 0.0), axis=-1, keepdims=True)
            hi = jnp.sum(jnp.where(low, 0.0, x), axis=-1, keepdims=True)
            return jnp.where(low, lo, hi) * (1.0 / RET_DV)
        d = o - half_mean(o)
        on = d * lax.rsqrt(half_mean(d * d) + EPS) * gnw_ref[:, sl_v]
        o_ref[:, sl_v] = (g_ref[:, sl_v].astype(F32) * on).astype(BF16)


def _retention(rq, rk, rv, g, t0, gn_w, layer, *, L):
    B, S, _ = rq.shape
    dec, qd, kd, rs = _retention_tables(L)
    blk = lambda b, i: (b, i, 0)
    st = lambda b, i: (b, 0, 0, 0)
    return pl.pallas_call(
        _ret_kernel,
        grid=(B, S // L),
        in_specs=[pl.BlockSpec((None, L, 256), blk), pl.BlockSpec((None, L, 256), blk),
                  pl.BlockSpec((None, L, 512), blk), pl.BlockSpec((None, L, 512), blk),
                  pl.BlockSpec((None, HEAD_PAIRS, LANES, LANES), st),
                  _const_spec(dec.shape), _const_spec(qd.shape), _const_spec(kd.shape),
                  _const_spec(rs.shape), _layer_spec(gn_w, layer)],
        out_specs=[pl.BlockSpec((None, L, 512), blk),
                   pl.BlockSpec((None, HEAD_PAIRS, LANES, LANES), st)],
        out_shape=[jax.ShapeDtypeStruct((B, S, 512), BF16),
                   jax.ShapeDtypeStruct((B, HEAD_PAIRS, LANES, LANES), F32)],
        compiler_params=_params("arbitrary", "arbitrary"),
        name="retention",
    )(rq, rk, rv, g, t0, dec, qd, kd, rs, gn_w)


def _post_kernel(h_ref, att_ref, ret_ref, p_ref, woa_ref, wor_ref, nf_ref, w1_ref, w2_ref,
                 np_ref, wg_ref, wp_ref, fin_ref, o_ref, *, final, fc):
    h = h_ref[...] + _dot(att_ref[...], woa_ref[...]) + _dot(ret_ref[...], wor_ref[...])
    a = _rms(h, nf_ref[...]).astype(BF16)
    acc = jnp.zeros_like(h)
    for c in range(D_FF // fc):
        u = _dot(a, w1_ref[:, c * fc:(c + 1) * fc])
        acc = acc + _dot(jnp.square(jnp.maximum(u, 0.0)).astype(BF16), w2_ref[c * fc:(c + 1) * fc, :])
    h = h + acc
    gate = jax.nn.sigmoid(_dot(_rms(h, np_ref[...]).astype(BF16), wg_ref[...]))
    h = h + _dot(p_ref[...].astype(BF16), wp_ref[...]) * gate
    o_ref[...] = _rms(h, fin_ref[...]) if final else h


def _post(h, att, ret, p, norm_ffn_w, norm_ple_w, final_norm_w, w, layer, *, tm, final):
    N = h.shape[0]
    row = lambda i: (i, 0)
    return pl.pallas_call(
        functools.partial(_post_kernel, final=final, fc=512),
        grid=(N // tm,),
        in_specs=[pl.BlockSpec((tm, D_MODEL), row), pl.BlockSpec((tm, 512), row),
                  pl.BlockSpec((tm, 512), row),
                  pl.BlockSpec((tm, PLE_DIM), lambda i: (layer * (N // tm) + i, 0)),
                  _layer_spec(w['woa'], layer), _layer_spec(w['wor'], layer), _layer_spec(norm_ffn_w, layer),
                  _layer_spec(w['w1'], layer), _layer_spec(w['w2'], layer), _layer_spec(norm_ple_w, layer),
                  _layer_spec(w['wg'], layer), _layer_spec(w['wp'], layer), _const_spec((1, D_MODEL))],
        out_specs=pl.BlockSpec((tm, D_MODEL), row),
        out_shape=jax.ShapeDtypeStruct((N, D_MODEL), F32),
        compiler_params=_params("arbitrary"),
        name="post",
    )(h, att, ret, p, w['woa'], w['wor'], norm_ffn_w, w['w1'], w['w2'], norm_ple_w, w['wg'], w['wp'],
      final_norm_w.reshape(1, -1))


PROMPT_TM = 512
RET_BLOCK = 256
EXPAND_TM = 1024


def kernel(x_prompt, x_sample, cache_ckv, cache_krope, state_ret, p_prompt, p_sample, norm_mix_w, w_in,
           q_norm_w, w_uq, kv_norm_w, w_ukv, ret_gn_w, w_out, norm_ffn_w, w_ff1, w_ff2, norm_ple_w,
           w_ple_gate, w_ple_proj, final_norm_w):
    B, S, D = x_prompt.shape
    Bd, Ld, _ = x_sample.shape
    depth, _, past, _ = cache_ckv.shape
    tm_p = min(PROMPT_TM, S)
    assert past % CHUNK == 0 and Ld <= CHUNK and S % tm_p == 0 and tm_p % CHUNK == 0
    tm_c = min(EXPAND_TM, Bd * past)
    Np, Ns = B * S, Bd * Ld

    w = _prep_weights(w_in, w_uq, w_ukv, w_out, w_ff1, w_ff2, w_ple_gate, w_ple_proj)
    norm_mix_w, q_norm_w, kv_norm_w, ret_gn_w, norm_ffn_w, norm_ple_w = map(
        _rows, (norm_mix_w, q_norm_w, kv_norm_w, ret_gn_w, norm_ffn_w, norm_ple_w))
    tab_p = _rope_table(jnp.arange(S, dtype=jnp.int32))
    tab_s = jnp.tile(_rope_table(past + jnp.arange(Ld, dtype=jnp.int32)), (1, Bd, 1))
    zero_state = jnp.zeros((B, HEAD_PAIRS, LANES, LANES), F32)
    pp = p_prompt.reshape(depth * Np, PLE_DIM)
    ps = p_sample.reshape(depth * Ns, PLE_DIM)
    cache_c = cache_ckv.reshape(depth * Bd * past, MLA_KV_LORA)
    cache_r = cache_krope.reshape(depth * Bd * past, MLA_ROPE)

    hp = x_prompt.reshape(Np, D)
    hs = x_sample.reshape(Ns, D)
    lat_p = lat_s = None
    st_p, st_s = [], []
    for i in range(depth):
        final = i == depth - 1

        q, k, v, ckv, kr, rq, rk, rv, rg = _proj(hp, tab_p, norm_mix_w, q_norm_w, kv_norm_w, w, i, lat_p,
                                                 tm=tm_p, n_pos=S // tm_p, n_rep=B, v_rows=True)
        lat_p = (ckv, kr)
        att = _attn_prompt(q.reshape(B, S, -1), k.reshape(B, S, -1),
                           v.reshape(B, S // tm_p, MLA_HEADS, V_ROWS, tm_p), t=tm_p)
        ret, st = _retention(rq.reshape(B, S, -1), rk.reshape(B, S, -1), rv.reshape(B, S, -1),
                             rg.reshape(B, S, -1), zero_state, ret_gn_w, i, L=min(RET_BLOCK, S))
        hp = _post(hp, att.reshape(Np, -1), ret.reshape(Np, -1), pp, norm_ffn_w, norm_ple_w, final_norm_w,
                   w, i, tm=tm_p, final=final)
        st_p.append(_slabs_to_state(st).astype(x_prompt.dtype))

        q, k, v, ckv, kr, rq, rk, rv, rg = _proj(hs, tab_s, norm_mix_w, q_norm_w, kv_norm_w, w, i, lat_s,
                                                 tm=Ns, n_pos=1, n_rep=1, v_rows=False)
        lat_s = (ckv, kr)
        kc, vc = _expand(cache_c, cache_r, w, i, tm=tm_c)
        att = _attn_sample(q.reshape(Bd, Ld, -1), kc.reshape(Bd, past, -1), vc.reshape(Bd, past, -1),
                           k.reshape(Bd, Ld, -1), v.reshape(Bd, Ld, -1))
        ret, st = _retention(rq.reshape(Bd, Ld, -1), rk.reshape(Bd, Ld, -1), rv.reshape(Bd, Ld, -1),
                             rg.reshape(Bd, Ld, -1), _state_to_slabs(state_ret[i]), ret_gn_w, i, L=Ld)
        hs = _post(hs, att.reshape(Ns, -1), ret.reshape(Ns, -1), ps, norm_ffn_w, norm_ple_w, final_norm_w,
                   w, i, tm=Ns, final=final)
        st_s.append(_slabs_to_state(st).astype(x_sample.dtype))

    return (hp.reshape(B, S, D), hs.reshape(Bd, Ld, D),
            lat_p[0].reshape(depth, B, S, -1), lat_p[1].reshape(depth, B, S, -1), jnp.stack(st_p),
            lat_s[0].reshape(depth, Bd, Ld, -1), lat_s[1].reshape(depth, Bd, Ld, -1), jnp.stack(st_s))
```

```python
import functools

import numpy as np
import jax
import jax.numpy as jnp
from jax import lax
from jax.experimental import pallas as pl
from jax.experimental.pallas import tpu as pltpu

D_MODEL = 1024
CHUNK = 64
PLE_DIM = 256
D_FF = 4 * D_MODEL
MLA_HEADS = 8
MLA_NOPE = 64
MLA_ROPE = 32
MLA_V = 64
MLA_Q_LORA = 384
MLA_KV_LORA = 256
MLA_SCALE = (MLA_NOPE + MLA_ROPE) ** -0.5
Q_SCALE = MLA_SCALE * float(np.log2(np.e))
RET_HEADS = 8
RET_DK = 32
RET_DV = 64
ROPE_THETA = 10000.0
EPS = 1e-6
NEG = -1e30
IN_SPLITS = (MLA_Q_LORA, MLA_KV_LORA, MLA_ROPE, RET_HEADS * RET_DK, RET_HEADS * RET_DK,
             RET_HEADS * RET_DV, RET_HEADS * RET_DV)

LANES = 128
HEAD_PAIRS = MLA_HEADS // 2
V_ROWS = MLA_V + 16
VMEM_LIMIT = 56 * 1024 * 1024

_C_Q, _C_C, _C_R, _C_RV, _C_RG, _C_KR, _C_END = 0, 384, 640, 1152, 1664, 2176, 2304

F32 = jnp.float32
BF16 = jnp.bfloat16


def _pad_cols(w, left, width):
    lead, n = w.shape[:-1], w.shape[-1]
    return jnp.concatenate([jnp.zeros((*lead, left), w.dtype), w,
                            jnp.zeros((*lead, width - left - n), w.dtype)], axis=-1)


def _prep_weights(w_in, w_uq, w_ukv, w_out, w_ff1, w_ff2, w_ple_gate, w_ple_proj):
    depth = w_in.shape[0]
    offs = np.cumsum((0,) + IN_SPLITS)
    q_lat, ckv, kr, rq, rk, rv, rg = [w_in[..., offs[n]:offs[n + 1]] for n in range(len(IN_SPLITS))]
    win = jnp.concatenate([q_lat, ckv, rq, rk, rv, rg, _pad_cols(kr, MLA_NOPE, LANES)], axis=-1)
    assert win.shape[-1] == _C_END
    uq = w_uq.reshape(depth, MLA_Q_LORA, MLA_HEADS, MLA_NOPE + MLA_ROPE)
    wq = _pad_cols(uq, 0, LANES).reshape(depth, MLA_Q_LORA, MLA_HEADS * LANES)
    ukv = w_ukv.reshape(depth, MLA_KV_LORA, MLA_HEADS, MLA_NOPE + MLA_V)
    wv = ukv[..., MLA_NOPE:].reshape(depth, MLA_KV_LORA, MLA_HEADS * MLA_V)
    place = np.zeros((MLA_ROPE, MLA_HEADS * LANES), np.float32)
    for h in range(MLA_HEADS):
        for r in range(MLA_ROPE):
            place[r, h * LANES + MLA_NOPE + r] = 1.0
    return dict(
        win=win.astype(BF16),
        wq=wq.astype(BF16),
        wk=_pad_cols(ukv[..., :MLA_NOPE], 0, LANES).reshape(depth, MLA_KV_LORA, MLA_HEADS * LANES).astype(BF16),
        wv=wv.astype(BF16),
        wvt=jnp.swapaxes(wv, 1, 2).astype(BF16),
        place=jnp.asarray(place, BF16),
        woa=w_out[:, :MLA_HEADS * MLA_V].astype(BF16),
        wor=w_out[:, MLA_HEADS * MLA_V:].astype(BF16),
        w1=w_ff1.astype(BF16),
        w2=w_ff2.astype(BF16),
        wg=w_ple_gate.astype(BF16),
        wp=w_ple_proj.astype(BF16),
    )


def _rope_table(pos):
    half = MLA_ROPE // 2
    inv = ROPE_THETA ** (-jnp.arange(half, dtype=F32) / half)
    ang = pos.astype(F32)[:, None] * inv[None, :]
    cos, sin = jnp.cos(ang), jnp.sin(ang)
    return jnp.stack([jnp.tile(cos, (1, LANES // half)), jnp.tile(sin, (1, LANES // half))])


def _retention_tables(L):
    log_gamma = np.log1p(-np.exp2(-5.0 - np.arange(RET_HEADS)))
    idx = np.arange(L, dtype=np.float64)
    diff = idx[:, None] - idx[None, :]
    dec = np.where(diff >= 0, np.exp(np.maximum(diff, 0.0)[None] * log_gamma[:, None, None]), 0.0)
    lg_lane = np.repeat(log_gamma, RET_DK).reshape(2, 1, LANES)
    qd = np.exp((idx + 1.0)[None, :, None] * lg_lane)
    kd = np.exp((L - 1.0 - idx)[None, :, None] * lg_lane)
    rs = np.broadcast_to(np.exp(L * lg_lane).reshape(2, LANES, 1), (2, LANES, LANES))
    return [jnp.asarray(x, F32) for x in (dec, qd, kd, rs)]


def _state_to_slabs(state):
    B = state.shape[0]
    t = jnp.zeros((B, HEAD_PAIRS, 4, RET_DK, 2, RET_DV), F32)
    for h in range(RET_HEADS):
        t = t.at[:, h // 2, h % 4, :, h % 2, :].set(state[:, h].astype(F32))
    return t.reshape(B, HEAD_PAIRS, LANES, LANES)


def _slabs_to_state(t):
    B = t.shape[0]
    t = t.reshape(B, HEAD_PAIRS, 4, RET_DK, 2, RET_DV)
    return jnp.stack([t[:, h // 2, h % 4, :, h % 2, :] for h in range(RET_HEADS)], axis=1)


def _rms(x, w):
    return x * lax.rsqrt(jnp.mean(x * x, axis=-1, keepdims=True) + EPS) * w


def _dot(a, b):
    return jnp.dot(a, b, preferred_element_type=F32)


def _dot_nt(a, b):
    return lax.dot_general(a, b, (((1,), (1,)), ((), ())), preferred_element_type=F32)


def _dot_tn(a, b):
    return lax.dot_general(a, b, (((0,), (0,)), ((), ())), preferred_element_type=F32)


def _const_spec(shape):
    zeros = (0,) * len(shape)
    return pl.BlockSpec(shape, lambda *_: zeros, pipeline_mode=pl.Buffered(1))


def _layer_spec(stacked, layer):
    zeros = (0,) * (stacked.ndim - 1)
    return pl.BlockSpec((None,) + stacked.shape[1:], lambda *_: (layer,) + zeros,
                        pipeline_mode=pl.Buffered(1))


def _rows(w):
    return w.reshape(w.shape[0], 1, -1)


def _params(*sem):
    return pltpu.CompilerParams(dimension_semantics=sem, vmem_limit_bytes=VMEM_LIMIT)


def _proj_kernel(h_ref, tab_ref, nw_ref, win_ref, qnw_ref, wq_ref, kvnw_ref, wk_ref, wv_ref,
                 *refs, v_rows, n_prev):
    if n_prev:
        ckv_prev_ref, kr_prev_ref, *refs = refs
    q_ref, k_ref, v_ref, ckv_ref, kr_ref, rq_ref, rk_ref, rv_ref, rg_ref = refs
    tm = h_ref.shape[0]
    cos, sin = tab_ref[0], tab_ref[1]
    lane = lax.broadcasted_iota(jnp.int32, (tm, LANES), 1)
    rope_lanes = (lane >= MLA_NOPE) & (lane < MLA_NOPE + MLA_ROPE)
    half = MLA_ROPE // 2
    first_half = lane % MLA_ROPE < half
    sin = jnp.where(first_half, -sin, sin)

    def swap_halves(x):
        return jnp.where(first_half, pltpu.roll(x, LANES - half, 1), pltpu.roll(x, half, 1))

    a = _rms(h_ref[...], nw_ref[...]).astype(BF16)

    ql = _rms(_dot(a, win_ref[:, _C_Q:_C_C]), qnw_ref[...]).astype(BF16)
    qz = _dot(ql, wq_ref[...])
    cq = jnp.where(lane < MLA_NOPE, Q_SCALE, jnp.where(rope_lanes, cos * Q_SCALE, 0.0))
    sq = jnp.where(rope_lanes, sin * Q_SCALE, 0.0)
    for h in range(MLA_HEADS):
        sl = slice(h * LANES, (h + 1) * LANES)
        q_ref[:, sl] = (qz[:, sl] * cq + swap_halves(qz[:, sl]) * sq).astype(BF16)

    c = _rms(_dot(a, win_ref[:, _C_C:_C_R]), kvnw_ref[...])
    if n_prev:
        ckv_ref[:n_prev] = ckv_prev_ref[...]
        kr_ref[:n_prev] = kr_prev_ref[...]
    ckv_ref[n_prev] = c
    cb = c.astype(BF16)
    if v_rows:
        vt = _dot_nt(wv_ref[...], cb)
        ones_row = jnp.where(lax.broadcasted_iota(jnp.int32, (V_ROWS - MLA_V, tm), 0) == 0, 1.0, 0.0)
        for h in range(MLA_HEADS):
            v_ref[h, :MLA_V, :] = vt[h * MLA_V:(h + 1) * MLA_V, :].astype(BF16)
            v_ref[h, MLA_V:, :] = ones_row.astype(BF16)
    else:
        v_ref[...] = _dot(cb, wv_ref[...]).astype(BF16)
    zkr = _dot(a, win_ref[:, _C_KR:_C_END])
    kslab = jnp.where(rope_lanes, zkr * cos + swap_halves(zkr) * sin, 0.0)
    kr_ref[n_prev] = kslab[:, MLA_NOPE:MLA_NOPE + MLA_ROPE]
    kn = _dot(cb, wk_ref[...])
    for h in range(MLA_HEADS):
        sl = slice(h * LANES, (h + 1) * LANES)
        k_ref[:, sl] = (kn[:, sl] + kslab).astype(BF16)

    zr = _dot(a, win_ref[:, _C_R:_C_RV])
    kscale = RET_DK ** -0.5
    for s in range(2):
        sl = slice(s * LANES, (s + 1) * LANES)
        zq, zk = zr[:, sl], zr[:, 256 + s * LANES:256 + (s + 1) * LANES]
        rq_ref[:, sl] = (zq * cos + swap_halves(zq) * sin).astype(BF16)
        rk_ref[:, sl] = ((zk * cos + swap_halves(zk) * sin) * kscale).astype(BF16)
    rv_ref[...] = _dot(a, win_ref[:, _C_RV:_C_RG]).astype(BF16)
    g = _dot(a, win_ref[:, _C_RG:_C_KR])
    rg_ref[...] = (g * jax.nn.sigmoid(g)).astype(BF16)


def _proj(h, tab, norm_w, q_norm_w, kv_norm_w, w, layer, prev, *, tm, n_pos, n_rep, v_rows):
    N = h.shape[0]
    row = lambda s, b: (b * n_pos + s, 0)
    stacked = lambda s, b: (0, b * n_pos + s, 0)
    outs = [(MLA_HEADS * LANES, BF16), (MLA_HEADS * LANES, BF16), (MLA_HEADS * MLA_V, BF16),
            (MLA_KV_LORA, F32), (MLA_ROPE, F32), (256, BF16), (256, BF16), (512, BF16), (512, BF16)]
    out_specs = [pl.BlockSpec((tm, c), row) for c, _ in outs]
    out_shape = [jax.ShapeDtypeStruct((N, c), dt) for c, dt in outs]
    for n in (3, 4):
        out_specs[n] = pl.BlockSpec((layer + 1, tm, outs[n][0]), stacked)
        out_shape[n] = jax.ShapeDtypeStruct((layer + 1, N, outs[n][0]), F32)
    wv = w['wv']
    if v_rows:
        out_specs[2] = pl.BlockSpec((None, MLA_HEADS, V_ROWS, tm), lambda s, b: (b * n_pos + s, 0, 0, 0))
        out_shape[2] = jax.ShapeDtypeStruct((N // tm, MLA_HEADS, V_ROWS, tm), BF16)
        wv = w['wvt']
    prev = () if prev is None else prev
    assert len(prev) == (2 if layer else 0)
    return pl.pallas_call(
        functools.partial(_proj_kernel, v_rows=v_rows, n_prev=layer),
        grid=(n_pos, n_rep),
        in_specs=[
            pl.BlockSpec((tm, D_MODEL), row),
            pl.BlockSpec((2, tm, LANES), lambda s, b: (0, s, 0)),
            _layer_spec(norm_w, layer), _layer_spec(w['win'], layer),
            _layer_spec(q_norm_w, layer), _layer_spec(w['wq'], layer),
            _layer_spec(kv_norm_w, layer), _layer_spec(w['wk'], layer), _layer_spec(wv, layer),
        ] + [pl.BlockSpec((layer, tm, x.shape[-1]), stacked) for x in prev],
        out_specs=out_specs,
        out_shape=out_shape,
        compiler_params=_params("arbitrary", "arbitrary"),
        name="proj",
    )(h, tab, norm_w, w['win'], q_norm_w, w['wq'], kv_norm_w, w['wk'], wv, *prev)


def _expand_kernel(c_ref, kr_ref, wk_ref, place_ref, wv_ref, k_ref, v_ref):
    cb = c_ref[...].astype(BF16)
    k_ref[...] = (_dot(cb, wk_ref[...]) + _dot(kr_ref[...].astype(BF16), place_ref[...])).astype(BF16)
    v_ref[...] = _dot(cb, wv_ref[...]).astype(BF16)


def _expand(ckv, krope, w, layer, *, tm):
    N = ckv.shape[0] // w['wk'].shape[0]
    row = lambda i: (i, 0)
    cached = lambda i: (layer * (N // tm) + i, 0)
    return pl.pallas_call(
        _expand_kernel,
        grid=(N // tm,),
        in_specs=[pl.BlockSpec((tm, MLA_KV_LORA), cached), pl.BlockSpec((tm, MLA_ROPE), cached),
                  _layer_spec(w['wk'], layer), _const_spec(w['place'].shape), _layer_spec(w['wv'], layer)],
        out_specs=[pl.BlockSpec((tm, MLA_HEADS * LANES), row), pl.BlockSpec((tm, MLA_HEADS * MLA_V), row)],
        out_shape=[jax.ShapeDtypeStruct((N, MLA_HEADS * LANES), BF16),
                   jax.ShapeDtypeStruct((N, MLA_HEADS * MLA_V), BF16)],
        compiler_params=_params("arbitrary"),
        name="expand",
    )(ckv, krope, w['wk'], w['place'], w['wv'])


def _visible_cap(t):
    kc = np.arange(t)[:, None] // CHUNK
    qc = np.arange(t)[None, :] // CHUNK
    return jnp.asarray(np.where(kc <= qc, np.finfo(np.float32).max, NEG), F32)


def _attn_prompt_kernel(q_ref, k_ref, vt_ref, cap_ref, o_ref, s_ref, acc_ref, m_ref, *, t, n_tiles):
    acc_ref[...] = jnp.zeros(acc_ref.shape, F32)
    m_ref[...] = jnp.full(m_ref.shape, NEG, F32)

    def produce(item, slot, masked):
        qi, kj = item
        q0 = pl.multiple_of(qi * t, t)
        k0 = pl.multiple_of(kj * t, t)
        out = []
        for hh in range(2):
            sl = slice(hh * LANES, (hh + 1) * LANES)
            s = _dot_nt(k_ref[pl.ds(k0, t), sl], q_ref[pl.ds(q0, t), sl])
            if masked:
                cols = []
                for lt in range(t // LANES):
                    r0 = (lt * LANES // CHUNK + 1) * CHUNK
                    col = s[:, lt * LANES:(lt + 1) * LANES]
                    capped = jnp.minimum(col[r0:], cap_ref[r0:, lt * LANES:(lt + 1) * LANES])
                    cols.append(jnp.concatenate([col[:r0], capped], axis=0))
                s = jnp.concatenate(cols, axis=1)
            s_ref[slot, hh] = s
            out.append(jnp.max(s, axis=0, keepdims=True))
        return tuple(out)

    def consume(smax, slot, item):
        qi, kj = item
        for hh in range(2):
            m = m_ref[qi, hh]
            m_new = jnp.maximum(m, smax[hh])
            p = jnp.exp2(s_ref[slot, hh] - m_new).astype(BF16)
            acc_ref[qi, hh] = jnp.exp2(m - m_new) * acc_ref[qi, hh] + _dot(vt_ref[kj, hh], p)
            m_ref[qi, hh] = m_new

    def run(first, count, advance, masked):
        if count == 0:
            return

        def step(carry, slot):
            smax, prev, cur = carry
            nxt = produce(cur, 1 - slot, masked)
            consume(smax, slot, prev)
            return nxt, cur, advance(cur)

        def steps(r, carry):
            for u in range(STEPS_PER_ITER):
                carry = step(carry, u % 2)
            return carry

        carry = lax.fori_loop(0, (count - 1) // STEPS_PER_ITER, steps,
                              (produce(first, 0, masked), first, advance(first)))
        left = (count - 1) % STEPS_PER_ITER
        for u in range(left):
            carry = step(carry, u % 2)
        consume(carry[0], left % 2, carry[1])

    def next_below(item):
        qi, kj = item
        last = kj == qi - 1
        return jnp.where(last, qi + 1, qi), jnp.where(last, 0, kj + 1)

    run((jnp.int32(1), jnp.int32(0)), n_tiles * (n_tiles - 1) // 2, next_below, False)
    run((jnp.int32(0), jnp.int32(0)), n_tiles, lambda item: (item[0] + 1, item[1] + 1), True)

    def finish(qi, carry):
        heads = [acc_ref[qi, hh, :MLA_V] * (1.0 / acc_ref[qi, hh, MLA_V:MLA_V + 1]) for hh in range(2)]
        o_ref[pl.ds(pl.multiple_of(qi * t, t), t), :] = jnp.concatenate(heads, axis=0).T.astype(BF16)
        return carry

    lax.fori_loop(0, n_tiles, finish, 0)


def _attn_prompt(q, k, vt, *, t):
    B, S, _ = q.shape
    n_tiles = S // t
    return pl.pallas_call(
        functools.partial(_attn_prompt_kernel, t=t, n_tiles=n_tiles),
        grid=(B, HEAD_PAIRS),
        in_specs=[pl.BlockSpec((None, S, 2 * LANES), lambda b, p: (b, 0, p)),
                  pl.BlockSpec((None, S, 2 * LANES), lambda b, p: (b, 0, p)),
                  pl.BlockSpec((None, n_tiles, 2, V_ROWS, t), lambda b, p: (b, 0, p, 0, 0)),
                  _const_spec((t, t))],
        out_specs=pl.BlockSpec((None, S, LANES), lambda b, p: (b, 0, p)),
        out_shape=jax.ShapeDtypeStruct((B, S, MLA_HEADS * MLA_V), BF16),
        scratch_shapes=[pltpu.VMEM((2, 2, t, t), F32), pltpu.VMEM((n_tiles, 2, V_ROWS, t), F32),
                        pltpu.VMEM((n_tiles, 2, 1, t), F32)],
        compiler_params=_params("arbitrary", "arbitrary"),
        name="attn_prompt",
    )(q, k, vt, _visible_cap(t))


def _attn_sample_kernel(q_ref, kc_ref, vc_ref, kn_ref, vn_ref, o_ref):
    L = q_ref.shape[0]
    outs = []
    for hh in range(2):
        sl = slice(hh * LANES, (hh + 1) * LANES)
        qh = q_ref[:, sl]
        sc = _dot_nt(qh, kc_ref[:, sl])
        sn = _dot_nt(qh, kn_ref[:, sl])
        m = jnp.maximum(jnp.max(sc, axis=-1, keepdims=True), jnp.max(sn, axis=-1, keepdims=True))
        pc, pn = jnp.exp2(sc - m), jnp.exp2(sn - m)
        l = jnp.sum(pc, axis=-1, keepdims=True) + jnp.sum(pn, axis=-1, keepdims=True)
        acc = _dot(pc.astype(BF16), vc_ref[...]) + _dot(pn.astype(BF16), vn_ref[...])
        outs.append(acc / l)
    lane = lax.broadcasted_iota(jnp.int32, (L, LANES), 1)
    o_ref[...] = jnp.where(lane < MLA_V, outs[0], outs[1]).astype(BF16)


def _attn_sample(q, kc, vc, kn, vn):
    B, L, _ = q.shape
    P = kc.shape[1]
    return pl.pallas_call(
        _attn_sample_kernel,
        grid=(B, HEAD_PAIRS),
        in_specs=[pl.BlockSpec((None, L, 2 * LANES), lambda b, p: (b, 0, p)),
                  pl.BlockSpec((None, P, 2 * LANES), lambda b, p: (b, 0, p)),
                  pl.BlockSpec((None, P, LANES), lambda b, p: (b, 0, p)),
                  pl.BlockSpec((None, L, 2 * LANES), lambda b, p: (b, 0, p)),
                  pl.BlockSpec((None, L, LANES), lambda b, p: (b, 0, p))],
        out_specs=pl.BlockSpec((None, L, LANES), lambda b, p: (b, 0, p)),
        out_shape=jax.ShapeDtypeStruct((B, L, MLA_HEADS * MLA_V), BF16),
        compiler_params=_params("arbitrary", "arbitrary"),
        name="attn_sample",
    )(q, kc, vc, kn, vn)


def _ret_kernel(rq_ref, rk_ref, rv_ref, g_ref, t0_ref, dec_ref, qd_ref, kd_ref, rs_ref, gnw_ref,
                o_ref, st_ref):
    L = rq_ref.shape[0]

    @pl.when(pl.program_id(1) == 0)
    def _():
        st_ref[...] = t0_ref[...]

    lane = lax.broadcasted_iota(jnp.int32, (L, LANES), 1)
    low = lane < RET_DV
    srow = lax.broadcasted_iota(jnp.int32, (LANES, LANES), 0) // RET_DK
    scol = lax.broadcasted_iota(jnp.int32, (LANES, LANES), 1) // RET_DV
    qk = [(rq_ref[:, s * LANES:(s + 1) * LANES], rk_ref[:, s * LANES:(s + 1) * LANES]) for s in range(2)]
    vps = [rv_ref[:, p * LANES:(p + 1) * LANES] for p in range(HEAD_PAIRS)]
    states = [st_ref[p] for p in range(HEAD_PAIRS)]

    cross = [_dot((qk[p // 2][0].astype(F32) * qd_ref[p // 2]).astype(BF16), states[p].astype(BF16))
             for p in range(HEAD_PAIRS)]
    raw = []
    for h in range(RET_HEADS):
        qs, ks = qk[h // 4]
        raw.append(_dot_nt(jnp.where(lane // RET_DK == h % 4, qs, jnp.zeros_like(qs)), ks))
    for p in range(HEAD_PAIRS):
        ks = qk[p // 2][1]
        kdec = (ks.astype(F32) * kd_ref[p // 2]).astype(BF16)
        new_state = rs_ref[p // 2] * states[p] + _dot_tn(kdec, vps[p])
        st_ref[p] = jnp.where(srow == 2 * (p % 2) + scol, new_state, 0.0)

    def half_mean(x):
        lo = jnp.sum(jnp.where(low, x, 0.0), axis=-1, keepdims=True)
        hi = jnp.sum(jnp.where(low, 0.0, x), axis=-1, keepdims=True)
        return jnp.where(low, lo, hi) * (1.0 / RET_DV)

    for p in range(HEAD_PAIRS):
        sl_v = slice(p * LANES, (p + 1) * LANES)
        o = cross[p]
        for c in range(2):
            h = 2 * p + c
            oh = _dot((raw[h] * dec_ref[h]).astype(BF16), vps[p])
            o = o + jnp.where(low if c == 0 else ~low, oh, 0.0)
        d = o - half_mean(o)
        on = d * lax.rsqrt(half_mean(d * d) + EPS) * gnw_ref[:, sl_v]
        o_ref[:, sl_v] = (g_ref[:, sl_v].astype(F32) * on).astype(BF16)


def _retention(rq, rk, rv, g, t0, gn_w, layer, *, L):
    B, S, _ = rq.shape
    dec, qd, kd, rs = _retention_tables(L)
    blk = lambda b, i: (b, i, 0)
    st = lambda b, i: (b, 0, 0, 0)
    return pl.pallas_call(
        _ret_kernel,
        grid=(B, S // L),
        in_specs=[pl.BlockSpec((None, L, 256), blk), pl.BlockSpec((None, L, 256), blk),
                  pl.BlockSpec((None, L, 512), blk), pl.BlockSpec((None, L, 512), blk),
                  pl.BlockSpec((None, HEAD_PAIRS, LANES, LANES), st),
                  _const_spec(dec.shape), _const_spec(qd.shape), _const_spec(kd.shape),
                  _const_spec(rs.shape), _layer_spec(gn_w, layer)],
        out_specs=[pl.BlockSpec((None, L, 512), blk),
                   pl.BlockSpec((None, HEAD_PAIRS, LANES, LANES), st)],
        out_shape=[jax.ShapeDtypeStruct((B, S, 512), BF16),
                   jax.ShapeDtypeStruct((B, HEAD_PAIRS, LANES, LANES), F32)],
        compiler_params=_params("arbitrary", "arbitrary"),
        name="retention",
    )(rq, rk, rv, g, t0, dec, qd, kd, rs, gn_w)


def _post_kernel(h_ref, att_ref, ret_ref, p_ref, woa_ref, wor_ref, nf_ref, w1_ref, w2_ref,
                 np_ref, wg_ref, wp_ref, fin_ref, o_ref, *, final, fc):
    h = h_ref[...] + _dot(att_ref[...], woa_ref[...]) + _dot(ret_ref[...], wor_ref[...])
    a = _rms(h, nf_ref[...]).astype(BF16)
    acc = jnp.zeros_like(h)
    for c in range(D_FF // fc):
        u = _dot(a, w1_ref[:, c * fc:(c + 1) * fc])
        acc = acc + _dot(jnp.square(jnp.maximum(u, 0.0)).astype(BF16), w2_ref[c * fc:(c + 1) * fc, :])
    h = h + acc
    gate = jax.nn.sigmoid(_dot(_rms(h, np_ref[...]).astype(BF16), wg_ref[...]))
    h = h + _dot(p_ref[...].astype(BF16), wp_ref[...]) * gate
    o_ref[...] = _rms(h, fin_ref[...]) if final else h


def _post(h, att, ret, p, norm_ffn_w, norm_ple_w, final_norm_w, w, layer, *, tm, final):
    N = h.shape[0]
    row = lambda i: (i, 0)
    return pl.pallas_call(
        functools.partial(_post_kernel, final=final, fc=512),
        grid=(N // tm,),
        in_specs=[pl.BlockSpec((tm, D_MODEL), row), pl.BlockSpec((tm, 512), row),
                  pl.BlockSpec((tm, 512), row),
                  pl.BlockSpec((tm, PLE_DIM), lambda i: (layer * (N // tm) + i, 0)),
                  _layer_spec(w['woa'], layer), _layer_spec(w['wor'], layer), _layer_spec(norm_ffn_w, layer),
                  _layer_spec(w['w1'], layer), _layer_spec(w['w2'], layer), _layer_spec(norm_ple_w, layer),
                  _layer_spec(w['wg'], layer), _layer_spec(w['wp'], layer), _const_spec((1, D_MODEL))],
        out_specs=pl.BlockSpec((tm, D_MODEL), row),
        out_shape=jax.ShapeDtypeStruct((N, D_MODEL), F32),
        compiler_params=_params("arbitrary"),
        name="post",
    )(h, att, ret, p, w['woa'], w['wor'], norm_ffn_w, w['w1'], w['w2'], norm_ple_w, w['wg'], w['wp'],
      final_norm_w.reshape(1, -1))


PROMPT_TM = 512
RET_BLOCK = 256
EXPAND_TM = 1024
STEPS_PER_ITER = 4


def kernel(x_prompt, x_sample, cache_ckv, cache_krope, state_ret, p_prompt, p_sample, norm_mix_w, w_in,
           q_norm_w, w_uq, kv_norm_w, w_ukv, ret_gn_w, w_out, norm_ffn_w, w_ff1, w_ff2, norm_ple_w,
           w_ple_gate, w_ple_proj, final_norm_w):
    B, S, D = x_prompt.shape
    Bd, Ld, _ = x_sample.shape
    depth, _, past, _ = cache_ckv.shape
    tm_p = min(PROMPT_TM, S)
    assert past % CHUNK == 0 and Ld <= CHUNK and S % tm_p == 0 and tm_p % CHUNK == 0
    tm_c = min(EXPAND_TM, Bd * past)
    Np, Ns = B * S, Bd * Ld

    w = _prep_weights(w_in, w_uq, w_ukv, w_out, w_ff1, w_ff2, w_ple_gate, w_ple_proj)
    norm_mix_w, q_norm_w, kv_norm_w, ret_gn_w, norm_ffn_w, norm_ple_w = map(
        _rows, (norm_mix_w, q_norm_w, kv_norm_w, ret_gn_w, norm_ffn_w, norm_ple_w))
    tab_p = _rope_table(jnp.arange(S, dtype=jnp.int32))
    tab_s = jnp.tile(_rope_table(past + jnp.arange(Ld, dtype=jnp.int32)), (1, Bd, 1))
    zero_state = jnp.zeros((B, HEAD_PAIRS, LANES, LANES), F32)
    pp = p_prompt.reshape(depth * Np, PLE_DIM)
    ps = p_sample.reshape(depth * Ns, PLE_DIM)
    cache_c = cache_ckv.reshape(depth * Bd * past, MLA_KV_LORA)
    cache_r = cache_krope.reshape(depth * Bd * past, MLA_ROPE)

    hp = x_prompt.reshape(Np, D)
    hs = x_sample.reshape(Ns, D)
    lat_p = lat_s = None
    st_p, st_s = [], []
    for i in range(depth):
        final = i == depth - 1

        q, k, v, ckv, kr, rq, rk, rv, rg = _proj(hp, tab_p, norm_mix_w, q_norm_w, kv_norm_w, w, i, lat_p,
                                                 tm=tm_p, n_pos=S // tm_p, n_rep=B, v_rows=True)
        lat_p = (ckv, kr)
        att = _attn_prompt(q.reshape(B, S, -1), k.reshape(B, S, -1),
                           v.reshape(B, S // tm_p, MLA_HEADS, V_ROWS, tm_p), t=tm_p)
        ret, st = _retention(rq.reshape(B, S, -1), rk.reshape(B, S, -1), rv.reshape(B, S, -1),
                             rg.reshape(B, S, -1), zero_state, ret_gn_w, i, L=min(RET_BLOCK, S))
        hp = _post(hp, att.reshape(Np, -1), ret.reshape(Np, -1), pp, norm_ffn_w, norm_ple_w, final_norm_w,
                   w, i, tm=tm_p, final=final)
        st_p.append(_slabs_to_state(st).astype(x_prompt.dtype))

        q, k, v, ckv, kr, rq, rk, rv, rg = _proj(hs, tab_s, norm_mix_w, q_norm_w, kv_norm_w, w, i, lat_s,
                                                 tm=Ns, n_pos=1, n_rep=1, v_rows=False)
        lat_s = (ckv, kr)
        kc, vc = _expand(cache_c, cache_r, w, i, tm=tm_c)
        att = _attn_sample(q.reshape(Bd, Ld, -1), kc.reshape(Bd, past, -1), vc.reshape(Bd, past, -1),
                           k.reshape(Bd, Ld, -1), v.reshape(Bd, Ld, -1))
        ret, st = _retention(rq.reshape(Bd, Ld, -1), rk.reshape(Bd, Ld, -1), rv.reshape(Bd, Ld, -1),
                             rg.reshape(Bd, Ld, -1), _state_to_slabs(state_ret[i]), ret_gn_w, i, L=Ld)
        hs = _post(hs, att.reshape(Ns, -1), ret.reshape(Ns, -1), ps, norm_ffn_w, norm_ple_w, final_norm_w,
                   w, i, tm=Ns, final=final)
        st_s.append(_slabs_to_state(st).astype(x_sample.dtype))

    return (hp.reshape(B, S, D), hs.reshape(Bd, Ld, D),
            lat_p[0].reshape(depth, B, S, -1), lat_p[1].reshape(depth, B, S, -1), jnp.stack(st_p),
            lat_s[0].reshape(depth, Bd, Ld, -1), lat_s[1].reshape(depth, Bd, Ld, -1), jnp.stack(st_s))
```

```python
import functools

import numpy as np
import jax
import jax.numpy as jnp
from jax import lax
from jax.experimental import pallas as pl
from jax.experimental.pallas import tpu as pltpu

D_MODEL = 1024
CHUNK = 64
PLE_DIM = 256
D_FF = 4 * D_MODEL
MLA_HEADS = 8
MLA_NOPE = 64
MLA_ROPE = 32
MLA_V = 64
MLA_Q_LORA = 384
MLA_KV_LORA = 256
MLA_SCALE = (MLA_NOPE + MLA_ROPE) ** -0.5
Q_SCALE = MLA_SCALE * float(np.log2(np.e))
RET_HEADS = 8
RET_DK = 32
RET_DV = 64
ROPE_THETA = 10000.0
EPS = 1e-6
NEG = -1e30
IN_SPLITS = (MLA_Q_LORA, MLA_KV_LORA, MLA_ROPE, RET_HEADS * RET_DK, RET_HEADS * RET_DK,
             RET_HEADS * RET_DV, RET_HEADS * RET_DV)

LANES = 128
HEAD_PAIRS = MLA_HEADS // 2
V_ROWS = MLA_V + 16
VMEM_LIMIT = 56 * 1024 * 1024

_C_Q, _C_C, _C_R, _C_RV, _C_RG, _C_KR, _C_END = 0, 384, 640, 1152, 1664, 2176, 2304

F32 = jnp.float32
BF16 = jnp.bfloat16


def _pad_cols(w, left, width):
    lead, n = w.shape[:-1], w.shape[-1]
    return jnp.concatenate([jnp.zeros((*lead, left), w.dtype), w,
                            jnp.zeros((*lead, width - left - n), w.dtype)], axis=-1)


def _prep_weights(w_in, w_uq, w_ukv, w_out, w_ff1, w_ff2, w_ple_gate, w_ple_proj):
    depth = w_in.shape[0]
    offs = np.cumsum((0,) + IN_SPLITS)
    q_lat, ckv, kr, rq, rk, rv, rg = [w_in[..., offs[n]:offs[n + 1]] for n in range(len(IN_SPLITS))]
    win = jnp.concatenate([q_lat, ckv, rq, rk, rv, rg, _pad_cols(kr, MLA_NOPE, LANES)], axis=-1)
    assert win.shape[-1] == _C_END
    uq = w_uq.reshape(depth, MLA_Q_LORA, MLA_HEADS, MLA_NOPE + MLA_ROPE)
    wq = _pad_cols(uq, 0, LANES).reshape(depth, MLA_Q_LORA, MLA_HEADS * LANES)
    ukv = w_ukv.reshape(depth, MLA_KV_LORA, MLA_HEADS, MLA_NOPE + MLA_V)
    wv = ukv[..., MLA_NOPE:].reshape(depth, MLA_KV_LORA, MLA_HEADS * MLA_V)
    return dict(
        win=win.astype(BF16),
        wq=wq.astype(BF16),
        wk=_pad_cols(ukv[..., :MLA_NOPE], 0, LANES).reshape(depth, MLA_KV_LORA, MLA_HEADS * LANES).astype(BF16),
        wv=wv.astype(BF16),
        wvt=jnp.swapaxes(wv, 1, 2).astype(BF16),
        woa=w_out[:, :MLA_HEADS * MLA_V].astype(BF16),
        wor=w_out[:, MLA_HEADS * MLA_V:].astype(BF16),
        w1=w_ff1.astype(BF16),
        w2=w_ff2.astype(BF16),
        wg=w_ple_gate.astype(BF16),
        wp=w_ple_proj.astype(BF16),
    )


def _rope_table(pos):
    half = MLA_ROPE // 2
    inv = ROPE_THETA ** (-jnp.arange(half, dtype=F32) / half)
    ang = pos.astype(F32)[:, None] * inv[None, :]
    cos, sin = jnp.cos(ang), jnp.sin(ang)
    return jnp.stack([jnp.tile(cos, (1, LANES // half)), jnp.tile(sin, (1, LANES // half))])


def _retention_tables(L):
    log_gamma = np.log1p(-np.exp2(-5.0 - np.arange(RET_HEADS)))
    idx = np.arange(L, dtype=np.float64)
    diff = idx[:, None] - idx[None, :]
    dec = np.where(diff >= 0, np.exp(np.maximum(diff, 0.0)[None] * log_gamma[:, None, None]), 0.0)
    lg_lane = np.repeat(log_gamma, RET_DK).reshape(2, 1, LANES)
    qd = np.exp((idx + 1.0)[None, :, None] * lg_lane)
    kd = np.exp((L - 1.0 - idx)[None, :, None] * lg_lane)
    rs = np.broadcast_to(np.exp(L * lg_lane).reshape(2, LANES, 1), (2, LANES, LANES))
    return [jnp.asarray(x, F32) for x in (dec, qd, kd, rs)]


def _state_to_slabs(state):
    B = state.shape[0]
    t = jnp.zeros((B, HEAD_PAIRS, 4, RET_DK, 2, RET_DV), F32)
    for h in range(RET_HEADS):
        t = t.at[:, h // 2, h % 4, :, h % 2, :].set(state[:, h].astype(F32))
    return t.reshape(B, HEAD_PAIRS, LANES, LANES)


def _slabs_to_state(t):
    B = t.shape[0]
    t = t.reshape(B, HEAD_PAIRS, 4, RET_DK, 2, RET_DV)
    return jnp.stack([t[:, h // 2, h % 4, :, h % 2, :] for h in range(RET_HEADS)], axis=1)


def _rms(x, w):
    return x * lax.rsqrt(jnp.mean(x * x, axis=-1, keepdims=True) + EPS) * w


def _dot(a, b):
    return jnp.dot(a, b, preferred_element_type=F32)


def _dot_nt(a, b):
    return lax.dot_general(a, b, (((1,), (1,)), ((), ())), preferred_element_type=F32)


def _dot_tn(a, b):
    return lax.dot_general(a, b, (((0,), (0,)), ((), ())), preferred_element_type=F32)


def _const_spec(shape):
    zeros = (0,) * len(shape)
    return pl.BlockSpec(shape, lambda *_: zeros, pipeline_mode=pl.Buffered(1))


def _layer_spec(stacked, layer):
    zeros = (0,) * (stacked.ndim - 1)
    return pl.BlockSpec((None,) + stacked.shape[1:], lambda *_: (layer,) + zeros,
                        pipeline_mode=pl.Buffered(1))


def _rows(w):
    return w.reshape(w.shape[0], 1, -1)


def _params(*sem):
    return pltpu.CompilerParams(dimension_semantics=sem, vmem_limit_bytes=VMEM_LIMIT)


def _proj_kernel(h_ref, tab_ref, nw_ref, win_ref, qnw_ref, wq_ref, kvnw_ref, wk_ref, wv_ref,
                 *refs, v_rows, n_prev):
    if n_prev:
        ckv_prev_ref, kr_prev_ref, *refs = refs
    q_ref, k_ref, v_ref, ckv_ref, kr_ref, rq_ref, rk_ref, rv_ref, rg_ref = refs
    tm = h_ref.shape[0]
    cos, sin = tab_ref[0], tab_ref[1]
    lane = lax.broadcasted_iota(jnp.int32, (tm, LANES), 1)
    rope_lanes = (lane >= MLA_NOPE) & (lane < MLA_NOPE + MLA_ROPE)
    half = MLA_ROPE // 2
    first_half = lane % MLA_ROPE < half
    sin = jnp.where(first_half, -sin, sin)

    def swap_halves(x):
        return jnp.where(first_half, pltpu.roll(x, LANES - half, 1), pltpu.roll(x, half, 1))

    a = _rms(h_ref[...], nw_ref[...]).astype(BF16)

    ql = _rms(_dot(a, win_ref[:, _C_Q:_C_C]), qnw_ref[...]).astype(BF16)
    qz = _dot(ql, wq_ref[...])
    cq = jnp.where(lane < MLA_NOPE, Q_SCALE, jnp.where(rope_lanes, cos * Q_SCALE, 0.0))
    sq = jnp.where(rope_lanes, sin * Q_SCALE, 0.0)
    for h in range(MLA_HEADS):
        sl = slice(h * LANES, (h + 1) * LANES)
        q_ref[:, sl] = (qz[:, sl] * cq + swap_halves(qz[:, sl]) * sq).astype(BF16)

    c = _rms(_dot(a, win_ref[:, _C_C:_C_R]), kvnw_ref[...])
    if n_prev:
        ckv_ref[:n_prev] = ckv_prev_ref[...]
        kr_ref[:n_prev] = kr_prev_ref[...]
    ckv_ref[n_prev] = c
    cb = c.astype(BF16)
    if v_rows:
        vt = _dot_nt(wv_ref[...], cb)
        ones_row = jnp.where(lax.broadcasted_iota(jnp.int32, (V_ROWS - MLA_V, tm), 0) == 0, 1.0, 0.0)
        for h in range(MLA_HEADS):
            v_ref[h, :MLA_V, :] = vt[h * MLA_V:(h + 1) * MLA_V, :].astype(BF16)
            v_ref[h, MLA_V:, :] = ones_row.astype(BF16)
    else:
        v_ref[...] = _dot(cb, wv_ref[...]).astype(BF16)
    zkr = _dot(a, win_ref[:, _C_KR:_C_END])
    kslab = jnp.where(rope_lanes, zkr * cos + swap_halves(zkr) * sin, 0.0)
    kr_ref[n_prev] = kslab[:, MLA_NOPE:MLA_NOPE + MLA_ROPE]
    kn = _dot(cb, wk_ref[...])
    for h in range(MLA_HEADS):
        sl = slice(h * LANES, (h + 1) * LANES)
        k_ref[:, sl] = (kn[:, sl] + kslab).astype(BF16)

    zr = _dot(a, win_ref[:, _C_R:_C_RV])
    kscale = RET_DK ** -0.5
    for s in range(2):
        sl = slice(s * LANES, (s + 1) * LANES)
        zq, zk = zr[:, sl], zr[:, 256 + s * LANES:256 + (s + 1) * LANES]
        rq_ref[:, sl] = (zq * cos + swap_halves(zq) * sin).astype(BF16)
        rk_ref[:, sl] = ((zk * cos + swap_halves(zk) * sin) * kscale).astype(BF16)
    rv_ref[...] = _dot(a, win_ref[:, _C_RV:_C_RG]).astype(BF16)
    g = _dot(a, win_ref[:, _C_RG:_C_KR])
    rg_ref[...] = (g * jax.nn.sigmoid(g)).astype(BF16)


def _proj(h, tab, norm_w, q_norm_w, kv_norm_w, w, layer, prev, *, tm, n_pos, n_rep, v_rows):
    N = h.shape[0]
    row = lambda s, b: (b * n_pos + s, 0)
    stacked = lambda s, b: (0, b * n_pos + s, 0)
    outs = [(MLA_HEADS * LANES, BF16), (MLA_HEADS * LANES, BF16), (MLA_HEADS * MLA_V, BF16),
            (MLA_KV_LORA, F32), (MLA_ROPE, F32), (256, BF16), (256, BF16), (512, BF16), (512, BF16)]
    out_specs = [pl.BlockSpec((tm, c), row) for c, _ in outs]
    out_shape = [jax.ShapeDtypeStruct((N, c), dt) for c, dt in outs]
    for n in (3, 4):
        out_specs[n] = pl.BlockSpec((layer + 1, tm, outs[n][0]), stacked)
        out_shape[n] = jax.ShapeDtypeStruct((layer + 1, N, outs[n][0]), F32)
    wv = w['wv']
    if v_rows:
        out_specs[2] = pl.BlockSpec((None, MLA_HEADS, V_ROWS, tm), lambda s, b: (b * n_pos + s, 0, 0, 0))
        out_shape[2] = jax.ShapeDtypeStruct((N // tm, MLA_HEADS, V_ROWS, tm), BF16)
        wv = w['wvt']
    prev = () if prev is None else prev
    assert len(prev) == (2 if layer else 0)
    return pl.pallas_call(
        functools.partial(_proj_kernel, v_rows=v_rows, n_prev=layer),
        grid=(n_pos, n_rep),
        in_specs=[
            pl.BlockSpec((tm, D_MODEL), row),
            pl.BlockSpec((2, tm, LANES), lambda s, b: (0, s, 0)),
            _layer_spec(norm_w, layer), _layer_spec(w['win'], layer),
            _layer_spec(q_norm_w, layer), _layer_spec(w['wq'], layer),
            _layer_spec(kv_norm_w, layer), _layer_spec(w['wk'], layer), _layer_spec(wv, layer),
        ] + [pl.BlockSpec((layer, tm, x.shape[-1]), stacked) for x in prev],
        out_specs=out_specs,
        out_shape=out_shape,
        compiler_params=_params("arbitrary", "arbitrary"),
        name="proj",
    )(h, tab, norm_w, w['win'], q_norm_w, w['wq'], kv_norm_w, w['wk'], wv, *prev)


def _visible_cap(t):
    kc = np.arange(t)[:, None] // CHUNK
    qc = np.arange(t)[None, :] // CHUNK
    return jnp.asarray(np.where(kc <= qc, np.finfo(np.float32).max, NEG), F32)


def _attn_prompt_kernel(q_ref, k_ref, vt_ref, cap_ref, o_ref, s_ref, acc_ref, m_ref, *, t, n_tiles):
    acc_ref[...] = jnp.zeros(acc_ref.shape, F32)
    m_ref[...] = jnp.full(m_ref.shape, NEG, F32)

    def produce(item, slot, masked):
        qi, kj = item
        q0 = pl.multiple_of(qi * t, t)
        k0 = pl.multiple_of(kj * t, t)
        out = []
        for hh in range(2):
            sl = slice(hh * LANES, (hh + 1) * LANES)
            s = _dot_nt(k_ref[pl.ds(k0, t), sl], q_ref[pl.ds(q0, t), sl])
            if masked:
                cols = []
                for lt in range(t // LANES):
                    r0 = (lt * LANES // CHUNK + 1) * CHUNK
                    col = s[:, lt * LANES:(lt + 1) * LANES]
                    capped = jnp.minimum(col[r0:], cap_ref[r0:, lt * LANES:(lt + 1) * LANES])
                    cols.append(jnp.concatenate([col[:r0], capped], axis=0))
                s = jnp.concatenate(cols, axis=1)
            s_ref[slot, hh] = s
            out.append(jnp.max(s, axis=0, keepdims=True))
        return tuple(out)

    def consume(smax, slot, item):
        qi, kj = item
        for hh in range(2):
            m = m_ref[qi, hh]
            m_new = jnp.maximum(m, smax[hh])
            p = jnp.exp2(s_ref[slot, hh] - m_new).astype(BF16)
            acc_ref[qi, hh] = jnp.exp2(m - m_new) * acc_ref[qi, hh] + _dot(vt_ref[kj, hh], p)
            m_ref[qi, hh] = m_new

    def run(first, count, advance, masked):
        if count == 0:
            return

        def step(carry, slot):
            smax, prev, cur = carry
            nxt = produce(cur, 1 - slot, masked)
            consume(smax, slot, prev)
            return nxt, cur, advance(cur)

        def steps(r, carry):
            for u in range(STEPS_PER_ITER):
                carry = step(carry, u % 2)
            return carry

        carry = lax.fori_loop(0, (count - 1) // STEPS_PER_ITER, steps,
                              (produce(first, 0, masked), first, advance(first)))
        left = (count - 1) % STEPS_PER_ITER
        for u in range(left):
            carry = step(carry, u % 2)
        consume(carry[0], left % 2, carry[1])

    def next_below(item):
        qi, kj = item
        last = kj == qi - 1
        return jnp.where(last, qi + 1, qi), jnp.where(last, 0, kj + 1)

    run((jnp.int32(1), jnp.int32(0)), n_tiles * (n_tiles - 1) // 2, next_below, False)
    run((jnp.int32(0), jnp.int32(0)), n_tiles, lambda item: (item[0] + 1, item[1] + 1), True)

    def finish(qi, carry):
        heads = [acc_ref[qi, hh, :MLA_V] * (1.0 / acc_ref[qi, hh, MLA_V:MLA_V + 1]) for hh in range(2)]
        o_ref[pl.ds(pl.multiple_of(qi * t, t), t), :] = jnp.concatenate(heads, axis=0).T.astype(BF16)
        return carry

    lax.fori_loop(0, n_tiles, finish, 0)


def _attn_prompt(q, k, vt, *, t):
    B, S, _ = q.shape
    n_tiles = S // t
    return pl.pallas_call(
        functools.partial(_attn_prompt_kernel, t=t, n_tiles=n_tiles),
        grid=(B, HEAD_PAIRS),
        in_specs=[pl.BlockSpec((None, S, 2 * LANES), lambda b, p: (b, 0, p)),
                  pl.BlockSpec((None, S, 2 * LANES), lambda b, p: (b, 0, p)),
                  pl.BlockSpec((None, n_tiles, 2, V_ROWS, t), lambda b, p: (b, 0, p, 0, 0)),
                  _const_spec((t, t))],
        out_specs=pl.BlockSpec((None, S, LANES), lambda b, p: (b, 0, p)),
        out_shape=jax.ShapeDtypeStruct((B, S, MLA_HEADS * MLA_V), BF16),
        scratch_shapes=[pltpu.VMEM((2, 2, t, t), F32), pltpu.VMEM((n_tiles, 2, V_ROWS, t), F32),
                        pltpu.VMEM((n_tiles, 2, 1, t), F32)],
        compiler_params=_params("arbitrary", "arbitrary"),
        name="attn_prompt",
    )(q, k, vt, _visible_cap(t))


def _attn_sample_kernel(q_ref, cn_ref, rn_ref, cc_ref, rc_ref, wk_ref, wv_ref, o_ref):
    L = q_ref.shape[0]
    q_lat, q_rot = [], []
    for h in range(MLA_HEADS):
        qh = q_ref[:, h * LANES:(h + 1) * LANES]
        q_lat.append(_dot_nt(qh, wk_ref[:, h * LANES:(h + 1) * LANES]))
        q_rot.append(qh[:, MLA_NOPE:MLA_NOPE + MLA_ROPE])
    q_lat = jnp.concatenate(q_lat, axis=0).astype(BF16)
    q_rot = jnp.concatenate(q_rot, axis=0)
    cc, rc = cc_ref[...].astype(BF16), rc_ref[...].astype(BF16)
    cn, rn = cn_ref[...].astype(BF16), rn_ref[...].astype(BF16)
    sc = _dot_nt(q_lat, cc) + _dot_nt(q_rot, rc)
    sn = _dot_nt(q_lat, cn) + _dot_nt(q_rot, rn)
    m = jnp.maximum(jnp.max(sc, axis=-1, keepdims=True), jnp.max(sn, axis=-1, keepdims=True))
    pc, pn = jnp.exp2(sc - m), jnp.exp2(sn - m)
    l = jnp.sum(pc, axis=-1, keepdims=True) + jnp.sum(pn, axis=-1, keepdims=True)
    o_lat = ((_dot(pc.astype(BF16), cc) + _dot(pn.astype(BF16), cn)) / l).astype(BF16)
    lane = lax.broadcasted_iota(jnp.int32, (L, LANES), 1)
    for p in range(HEAD_PAIRS):
        wvp = wv_ref[:, p * LANES:(p + 1) * LANES]
        even = _dot(o_lat[2 * p * L:(2 * p + 1) * L], wvp)
        odd = _dot(o_lat[(2 * p + 1) * L:(2 * p + 2) * L], wvp)
        o_ref[:, p * LANES:(p + 1) * LANES] = jnp.where(lane < MLA_V, even, odd).astype(BF16)


def _attn_sample(q, lat_new, cache_c, cache_r, w, layer, *, past):
    B, L, _ = q.shape
    new = lambda b: (layer, b, 0)
    old = lambda b: (layer * B + b, 0)
    return pl.pallas_call(
        _attn_sample_kernel,
        grid=(B,),
        in_specs=[pl.BlockSpec((None, L, MLA_HEADS * LANES), lambda b: (b, 0, 0)),
                  pl.BlockSpec((None, L, MLA_KV_LORA), new), pl.BlockSpec((None, L, MLA_ROPE), new),
                  pl.BlockSpec((past, MLA_KV_LORA), old), pl.BlockSpec((past, MLA_ROPE), old),
                  _layer_spec(w['wk'], layer), _layer_spec(w['wv'], layer)],
        out_specs=pl.BlockSpec((None, L, MLA_HEADS * MLA_V), lambda b: (b, 0, 0)),
        out_shape=jax.ShapeDtypeStruct((B, L, MLA_HEADS * MLA_V), BF16),
        compiler_params=_params("arbitrary"),
        name="attn_sample",
    )(q, *lat_new, cache_c, cache_r, w['wk'], w['wv'])


def _ret_kernel(rq_ref, rk_ref, rv_ref, g_ref, t0_ref, dec_ref, qd_ref, kd_ref, rs_ref, gnw_ref,
                o_ref, st_ref):
    L = rq_ref.shape[0]

    @pl.when(pl.program_id(1) == 0)
    def _():
        st_ref[...] = t0_ref[...]

    lane = lax.broadcasted_iota(jnp.int32, (L, LANES), 1)
    low = lane < RET_DV
    srow = lax.broadcasted_iota(jnp.int32, (LANES, LANES), 0) // RET_DK
    scol = lax.broadcasted_iota(jnp.int32, (LANES, LANES), 1) // RET_DV
    qk = [(rq_ref[:, s * LANES:(s + 1) * LANES], rk_ref[:, s * LANES:(s + 1) * LANES]) for s in range(2)]
    vps = [rv_ref[:, p * LANES:(p + 1) * LANES] for p in range(HEAD_PAIRS)]
    states = [st_ref[p] for p in range(HEAD_PAIRS)]

    cross = [_dot((qk[p // 2][0].astype(F32) * qd_ref[p // 2]).astype(BF16), states[p].astype(BF16))
             for p in range(HEAD_PAIRS)]
    raw = []
    for h in range(RET_HEADS):
        qs, ks = qk[h // 4]
        raw.append(_dot_nt(jnp.where(lane // RET_DK == h % 4, qs, jnp.zeros_like(qs)), ks))
    for p in range(HEAD_PAIRS):
        ks = qk[p // 2][1]
        kdec = (ks.astype(F32) * kd_ref[p // 2]).astype(BF16)
        new_state = rs_ref[p // 2] * states[p] + _dot_tn(kdec, vps[p])
        st_ref[p] = jnp.where(srow == 2 * (p % 2) + scol, new_state, 0.0)

    def half_mean(x):
        lo = jnp.sum(jnp.where(low, x, 0.0), axis=-1, keepdims=True)
        hi = jnp.sum(jnp.where(low, 0.0, x), axis=-1, keepdims=True)
        return jnp.where(low, lo, hi) * (1.0 / RET_DV)

    for p in range(HEAD_PAIRS):
        sl_v = slice(p * LANES, (p + 1) * LANES)
        o = cross[p]
        for c in range(2):
            h = 2 * p + c
            oh = _dot((raw[h] * dec_ref[h]).astype(BF16), vps[p])
            o = o + jnp.where(low if c == 0 else ~low, oh, 0.0)
        d = o - half_mean(o)
        on = d * lax.rsqrt(half_mean(d * d) + EPS) * gnw_ref[:, sl_v]
        o_ref[:, sl_v] = (g_ref[:, sl_v].astype(F32) * on).astype(BF16)


def _retention(rq, rk, rv, g, t0, gn_w, layer, *, L):
    B, S, _ = rq.shape
    dec, qd, kd, rs = _retention_tables(L)
    blk = lambda b, i: (b, i, 0)
    st = lambda b, i: (b, 0, 0, 0)
    return pl.pallas_call(
        _ret_kernel,
        grid=(B, S // L),
        in_specs=[pl.BlockSpec((None, L, 256), blk), pl.BlockSpec((None, L, 256), blk),
                  pl.BlockSpec((None, L, 512), blk), pl.BlockSpec((None, L, 512), blk),
                  pl.BlockSpec((None, HEAD_PAIRS, LANES, LANES), st),
                  _const_spec(dec.shape), _const_spec(qd.shape), _const_spec(kd.shape),
                  _const_spec(rs.shape), _layer_spec(gn_w, layer)],
        out_specs=[pl.BlockSpec((None, L, 512), blk),
                   pl.BlockSpec((None, HEAD_PAIRS, LANES, LANES), st)],
        out_shape=[jax.ShapeDtypeStruct((B, S, 512), BF16),
                   jax.ShapeDtypeStruct((B, HEAD_PAIRS, LANES, LANES), F32)],
        compiler_params=_params("arbitrary", "arbitrary"),
        name="retention",
    )(rq, rk, rv, g, t0, dec, qd, kd, rs, gn_w)


def _post_kernel(h_ref, att_ref, ret_ref, p_ref, woa_ref, wor_ref, nf_ref, w1_ref, w2_ref,
                 np_ref, wg_ref, wp_ref, fin_ref, o_ref, *, final, fc):
    h = h_ref[...] + _dot(att_ref[...], woa_ref[...]) + _dot(ret_ref[...], wor_ref[...])
    a = _rms(h, nf_ref[...]).astype(BF16)
    acc = jnp.zeros_like(h)
    for c in range(D_FF // fc):
        u = _dot(a, w1_ref[:, c * fc:(c + 1) * fc])
        acc = acc + _dot(jnp.square(jnp.maximum(u, 0.0)).astype(BF16), w2_ref[c * fc:(c + 1) * fc, :])
    h = h + acc
    gate = jax.nn.sigmoid(_dot(_rms(h, np_ref[...]).astype(BF16), wg_ref[...]))
    h = h + _dot(p_ref[...].astype(BF16), wp_ref[...]) * gate
    o_ref[...] = _rms(h, fin_ref[...]) if final else h


def _post(h, att, ret, p, norm_ffn_w, norm_ple_w, final_norm_w, w, layer, *, tm, final):
    N = h.shape[0]
    row = lambda i: (i, 0)
    return pl.pallas_call(
        functools.partial(_post_kernel, final=final, fc=512),
        grid=(N // tm,),
        in_specs=[pl.BlockSpec((tm, D_MODEL), row), pl.BlockSpec((tm, 512), row),
                  pl.BlockSpec((tm, 512), row),
                  pl.BlockSpec((tm, PLE_DIM), lambda i: (layer * (N // tm) + i, 0)),
                  _layer_spec(w['woa'], layer), _layer_spec(w['wor'], layer), _layer_spec(norm_ffn_w, layer),
                  _layer_spec(w['w1'], layer), _layer_spec(w['w2'], layer), _layer_spec(norm_ple_w, layer),
                  _layer_spec(w['wg'], layer), _layer_spec(w['wp'], layer), _const_spec((1, D_MODEL))],
        out_specs=pl.BlockSpec((tm, D_MODEL), row),
        out_shape=jax.ShapeDtypeStruct((N, D_MODEL), F32),
        compiler_params=_params("arbitrary"),
        name="post",
    )(h, att, ret, p, w['woa'], w['wor'], norm_ffn_w, w['w1'], w['w2'], norm_ple_w, w['wg'], w['wp'],
      final_norm_w.reshape(1, -1))


PROMPT_TM = 512
RET_BLOCK = 256
STEPS_PER_ITER = 8


def kernel(x_prompt, x_sample, cache_ckv, cache_krope, state_ret, p_prompt, p_sample, norm_mix_w, w_in,
           q_norm_w, w_uq, kv_norm_w, w_ukv, ret_gn_w, w_out, norm_ffn_w, w_ff1, w_ff2, norm_ple_w,
           w_ple_gate, w_ple_proj, final_norm_w):
    B, S, D = x_prompt.shape
    Bd, Ld, _ = x_sample.shape
    depth, _, past, _ = cache_ckv.shape
    tm_p = min(PROMPT_TM, S)
    assert past % CHUNK == 0 and Ld <= CHUNK and S % tm_p == 0 and tm_p % CHUNK == 0
    Np, Ns = B * S, Bd * Ld

    w = _prep_weights(w_in, w_uq, w_ukv, w_out, w_ff1, w_ff2, w_ple_gate, w_ple_proj)
    norm_mix_w, q_norm_w, kv_norm_w, ret_gn_w, norm_ffn_w, norm_ple_w = map(
        _rows, (norm_mix_w, q_norm_w, kv_norm_w, ret_gn_w, norm_ffn_w, norm_ple_w))
    tab_p = _rope_table(jnp.arange(S, dtype=jnp.int32))
    tab_s = jnp.tile(_rope_table(past + jnp.arange(Ld, dtype=jnp.int32)), (1, Bd, 1))
    zero_state = jnp.zeros((B, HEAD_PAIRS, LANES, LANES), F32)
    pp = p_prompt.reshape(depth * Np, PLE_DIM)
    ps = p_sample.reshape(depth * Ns, PLE_DIM)
    cache_c = cache_ckv.reshape(depth * Bd * past, MLA_KV_LORA)
    cache_r = cache_krope.reshape(depth * Bd * past, MLA_ROPE)

    hp = x_prompt.reshape(Np, D)
    hs = x_sample.reshape(Ns, D)
    lat_p = lat_s = None
    st_p, st_s = [], []
    for i in range(depth):
        final = i == depth - 1

        q, k, v, ckv, kr, rq, rk, rv, rg = _proj(hp, tab_p, norm_mix_w, q_norm_w, kv_norm_w, w, i, lat_p,
                                                 tm=tm_p, n_pos=S // tm_p, n_rep=B, v_rows=True)
        lat_p = (ckv, kr)
        att = _attn_prompt(q.reshape(B, S, -1), k.reshape(B, S, -1),
                           v.reshape(B, S // tm_p, MLA_HEADS, V_ROWS, tm_p), t=tm_p)
        ret, st = _retention(rq.reshape(B, S, -1), rk.reshape(B, S, -1), rv.reshape(B, S, -1),
                             rg.reshape(B, S, -1), zero_state, ret_gn_w, i, L=min(RET_BLOCK, S))
        hp = _post(hp, att.reshape(Np, -1), ret.reshape(Np, -1), pp, norm_ffn_w, norm_ple_w, final_norm_w,
                   w, i, tm=tm_p, final=final)
        st_p.append(_slabs_to_state(st).astype(x_prompt.dtype))

        q, k, v, ckv, kr, rq, rk, rv, rg = _proj(hs, tab_s, norm_mix_w, q_norm_w, kv_norm_w, w, i, lat_s,
                                                 tm=Ns, n_pos=1, n_rep=1, v_rows=False)
        lat_s = (ckv, kr)
        att = _attn_sample(q.reshape(Bd, Ld, -1), lat_s, cache_c, cache_r, w, i, past=past)
        ret, st = _retention(rq.reshape(Bd, Ld, -1), rk.reshape(Bd, Ld, -1), rv.reshape(Bd, Ld, -1),
                             rg.reshape(Bd, Ld, -1), _state_to_slabs(state_ret[i]), ret_gn_w, i, L=Ld)
        hs = _post(hs, att.reshape(Ns, -1), ret.reshape(Ns, -1), ps, norm_ffn_w, norm_ple_w, final_norm_w,
                   w, i, tm=Ns, final=final)
        st_s.append(_slabs_to_state(st).astype(x_sample.dtype))

    return (hp.reshape(B, S, D), hs.reshape(Bd, Ld, D),
            lat_p[0].reshape(depth, B, S, -1), lat_p[1].reshape(depth, B, S, -1), jnp.stack(st_p),
            lat_s[0].reshape(depth, Bd, Ld, -1), lat_s[1].reshape(depth, Bd, Ld, -1), jnp.stack(st_s))
```

```python
import functools

import numpy as np
import jax
import jax.numpy as jnp
from jax import lax
from jax.experimental import pallas as pl
from jax.experimental.pallas import tpu as pltpu

D_MODEL = 1024
CHUNK = 64
PLE_DIM = 256
D_FF = 4 * D_MODEL
MLA_HEADS = 8
MLA_NOPE = 64
MLA_ROPE = 32
MLA_V = 64
MLA_Q_LORA = 384
MLA_KV_LORA = 256
MLA_SCALE = (MLA_NOPE + MLA_ROPE) ** -0.5
Q_SCALE = MLA_SCALE * float(np.log2(np.e))
RET_HEADS = 8
RET_DK = 32
RET_DV = 64
ROPE_THETA = 10000.0
EPS = 1e-6
NEG = -1e30
IN_SPLITS = (MLA_Q_LORA, MLA_KV_LORA, MLA_ROPE, RET_HEADS * RET_DK, RET_HEADS * RET_DK,
             RET_HEADS * RET_DV, RET_HEADS * RET_DV)

LANES = 128
HEAD_PAIRS = MLA_HEADS // 2
V_ROWS = MLA_V + 16
VMEM_LIMIT = 56 * 1024 * 1024

_C_Q, _C_C, _C_R, _C_RV, _C_RG, _C_KR, _C_END = 0, 384, 640, 1152, 1664, 2176, 2304

F32 = jnp.float32
BF16 = jnp.bfloat16


def _pad_cols(w, left, width):
    lead, n = w.shape[:-1], w.shape[-1]
    return jnp.concatenate([jnp.zeros((*lead, left), w.dtype), w,
                            jnp.zeros((*lead, width - left - n), w.dtype)], axis=-1)


def _prep_weights(w_in, w_uq, w_ukv, w_out, w_ff1, w_ff2, w_ple_gate, w_ple_proj):
    depth = w_in.shape[0]
    offs = np.cumsum((0,) + IN_SPLITS)
    q_lat, ckv, kr, rq, rk, rv, rg = [w_in[..., offs[n]:offs[n + 1]] for n in range(len(IN_SPLITS))]
    win = jnp.concatenate([q_lat, ckv, rq, rk, rv, rg, _pad_cols(kr, MLA_NOPE, LANES)], axis=-1)
    assert win.shape[-1] == _C_END
    uq = w_uq.reshape(depth, MLA_Q_LORA, MLA_HEADS, MLA_NOPE + MLA_ROPE)
    wq = _pad_cols(uq, 0, LANES).reshape(depth, MLA_Q_LORA, MLA_HEADS * LANES)
    ukv = w_ukv.reshape(depth, MLA_KV_LORA, MLA_HEADS, MLA_NOPE + MLA_V)
    wv = ukv[..., MLA_NOPE:].reshape(depth, MLA_KV_LORA, MLA_HEADS * MLA_V)
    return dict(
        win=win.astype(BF16),
        wq=wq.astype(BF16),
        wk=_pad_cols(ukv[..., :MLA_NOPE], 0, LANES).reshape(depth, MLA_KV_LORA, MLA_HEADS * LANES).astype(BF16),
        wv=wv.astype(BF16),
        wvt=jnp.swapaxes(wv, 1, 2).astype(BF16),
        woa=w_out[:, :MLA_HEADS * MLA_V].astype(BF16),
        wor=w_out[:, MLA_HEADS * MLA_V:].astype(BF16),
        w1=w_ff1.astype(BF16),
        w2=w_ff2.astype(BF16),
        wg=w_ple_gate.astype(BF16),
        wp=w_ple_proj.astype(BF16),
    )


def _rope_table(pos):
    half = MLA_ROPE // 2
    inv = ROPE_THETA ** (-jnp.arange(half, dtype=F32) / half)
    ang = pos.astype(F32)[:, None] * inv[None, :]
    cos, sin = jnp.cos(ang), jnp.sin(ang)
    return jnp.stack([jnp.tile(cos, (1, LANES // half)), jnp.tile(sin, (1, LANES // half))])


def _retention_tables(L):
    log_gamma = np.log1p(-np.exp2(-5.0 - np.arange(RET_HEADS)))
    idx = np.arange(L, dtype=np.float64)
    diff = idx[:, None] - idx[None, :]
    dec = np.where(diff >= 0, np.exp(np.maximum(diff, 0.0)[None] * log_gamma[:, None, None]), 0.0)
    lg_lane = np.repeat(log_gamma, RET_DK).reshape(2, 1, LANES)
    qd = np.exp((idx + 1.0)[None, :, None] * lg_lane)
    kd = np.exp((L - 1.0 - idx)[None, :, None] * lg_lane)
    rs = np.broadcast_to(np.exp(L * lg_lane).reshape(2, LANES, 1), (2, LANES, LANES))
    return [jnp.asarray(x, F32) for x in (dec, qd, kd, rs)]


def _rms(x, w):
    return x * lax.rsqrt(jnp.mean(x * x, axis=-1, keepdims=True) + EPS) * w


def _group_norm_pair(o, gain):
    low = lax.broadcasted_iota(jnp.int32, o.shape, 1) < RET_DV

    def half_mean(x):
        lo = jnp.sum(jnp.where(low, x, 0.0), axis=-1, keepdims=True)
        hi = jnp.sum(jnp.where(low, 0.0, x), axis=-1, keepdims=True)
        return jnp.where(low, lo, hi) * (1.0 / RET_DV)

    d = o - half_mean(o)
    return d * lax.rsqrt(half_mean(d * d) + EPS) * gain


def _dot(a, b):
    return jnp.dot(a, b, preferred_element_type=F32)


def _dot_nt(a, b):
    return lax.dot_general(a, b, (((1,), (1,)), ((), ())), preferred_element_type=F32)


def _dot_tn(a, b):
    return lax.dot_general(a, b, (((0,), (0,)), ((), ())), preferred_element_type=F32)


def _const_spec(shape):
    zeros = (0,) * len(shape)
    return pl.BlockSpec(shape, lambda *_: zeros, pipeline_mode=pl.Buffered(1))


def _layer_spec(stacked, layer):
    zeros = (0,) * (stacked.ndim - 1)
    return pl.BlockSpec((None,) + stacked.shape[1:], lambda *_: (layer,) + zeros,
                        pipeline_mode=pl.Buffered(1))


def _rows(w):
    return w.reshape(w.shape[0], 1, -1)


def _params(*sem):
    return pltpu.CompilerParams(dimension_semantics=sem, vmem_limit_bytes=VMEM_LIMIT)


def _proj_kernel(h_ref, tab_ref, nw_ref, win_ref, qnw_ref, wq_ref, kvnw_ref, wk_ref, wv_ref,
                 *refs, v_rows, n_prev):
    if n_prev:
        ckv_prev_ref, kr_prev_ref, *refs = refs
    q_ref, k_ref, v_ref, ckv_ref, kr_ref, rq_ref, rk_ref, rv_ref, rg_ref = refs
    tm = h_ref.shape[0]
    cos, sin = tab_ref[0], tab_ref[1]
    lane = lax.broadcasted_iota(jnp.int32, (tm, LANES), 1)
    rope_lanes = (lane >= MLA_NOPE) & (lane < MLA_NOPE + MLA_ROPE)
    half = MLA_ROPE // 2
    first_half = lane % MLA_ROPE < half
    sin = jnp.where(first_half, -sin, sin)

    def swap_halves(x):
        return jnp.where(first_half, pltpu.roll(x, LANES - half, 1), pltpu.roll(x, half, 1))

    a = _rms(h_ref[...], nw_ref[...]).astype(BF16)

    ql = _rms(_dot(a, win_ref[:, _C_Q:_C_C]), qnw_ref[...]).astype(BF16)
    qz = _dot(ql, wq_ref[...])
    cq = jnp.where(lane < MLA_NOPE, Q_SCALE, jnp.where(rope_lanes, cos * Q_SCALE, 0.0))
    sq = jnp.where(rope_lanes, sin * Q_SCALE, 0.0)
    for h in range(MLA_HEADS):
        sl = slice(h * LANES, (h + 1) * LANES)
        q_ref[:, sl] = (qz[:, sl] * cq + swap_halves(qz[:, sl]) * sq).astype(BF16)

    c = _rms(_dot(a, win_ref[:, _C_C:_C_R]), kvnw_ref[...])
    if n_prev:
        ckv_ref[:n_prev] = ckv_prev_ref[...]
        kr_ref[:n_prev] = kr_prev_ref[...]
    ckv_ref[n_prev] = c
    cb = c.astype(BF16)
    if v_rows:
        vt = _dot_nt(wv_ref[...], cb)
        ones_row = jnp.where(lax.broadcasted_iota(jnp.int32, (V_ROWS - MLA_V, tm), 0) == 0, 1.0, 0.0)
        for h in range(MLA_HEADS):
            v_ref[h, :MLA_V, :] = vt[h * MLA_V:(h + 1) * MLA_V, :].astype(BF16)
            v_ref[h, MLA_V:, :] = ones_row.astype(BF16)
    else:
        v_ref[...] = _dot(cb, wv_ref[...]).astype(BF16)
    zkr = _dot(a, win_ref[:, _C_KR:_C_END])
    kslab = jnp.where(rope_lanes, zkr * cos + swap_halves(zkr) * sin, 0.0)
    kr_ref[n_prev] = kslab[:, MLA_NOPE:MLA_NOPE + MLA_ROPE]
    kn = _dot(cb, wk_ref[...])
    for h in range(MLA_HEADS):
        sl = slice(h * LANES, (h + 1) * LANES)
        k_ref[:, sl] = (kn[:, sl] + kslab).astype(BF16)

    zr = _dot(a, win_ref[:, _C_R:_C_RV])
    kscale = RET_DK ** -0.5
    for s in range(2):
        sl = slice(s * LANES, (s + 1) * LANES)
        zq, zk = zr[:, sl], zr[:, 256 + s * LANES:256 + (s + 1) * LANES]
        rq_ref[:, sl] = (zq * cos + swap_halves(zq) * sin).astype(BF16)
        rk_ref[:, sl] = ((zk * cos + swap_halves(zk) * sin) * kscale).astype(BF16)
    rv_ref[...] = _dot(a, win_ref[:, _C_RV:_C_RG]).astype(BF16)
    g = _dot(a, win_ref[:, _C_RG:_C_KR])
    rg_ref[...] = (g * jax.nn.sigmoid(g)).astype(BF16)


def _proj(h, tab, norm_w, q_norm_w, kv_norm_w, w, layer, prev, *, tm, n_pos, n_rep, v_rows):
    N = h.shape[0]
    row = lambda s, b: (b * n_pos + s, 0)
    stacked = lambda s, b: (0, b * n_pos + s, 0)
    outs = [(MLA_HEADS * LANES, BF16), (MLA_HEADS * LANES, BF16), (MLA_HEADS * MLA_V, BF16),
            (MLA_KV_LORA, F32), (MLA_ROPE, F32), (256, BF16), (256, BF16), (512, BF16), (512, BF16)]
    out_specs = [pl.BlockSpec((tm, c), row) for c, _ in outs]
    out_shape = [jax.ShapeDtypeStruct((N, c), dt) for c, dt in outs]
    for n in (3, 4):
        out_specs[n] = pl.BlockSpec((layer + 1, tm, outs[n][0]), stacked)
        out_shape[n] = jax.ShapeDtypeStruct((layer + 1, N, outs[n][0]), F32)
    wv = w['wv']
    if v_rows:
        out_specs[2] = pl.BlockSpec((None, MLA_HEADS, V_ROWS, tm), lambda s, b: (b * n_pos + s, 0, 0, 0))
        out_shape[2] = jax.ShapeDtypeStruct((N // tm, MLA_HEADS, V_ROWS, tm), BF16)
        wv = w['wvt']
    prev = () if prev is None else prev
    assert len(prev) == (2 if layer else 0)
    return pl.pallas_call(
        functools.partial(_proj_kernel, v_rows=v_rows, n_prev=layer),
        grid=(n_pos, n_rep),
        in_specs=[
            pl.BlockSpec((tm, D_MODEL), row),
            pl.BlockSpec((2, tm, LANES), lambda s, b: (0, s, 0)),
            _layer_spec(norm_w, layer), _layer_spec(w['win'], layer),
            _layer_spec(q_norm_w, layer), _layer_spec(w['wq'], layer),
            _layer_spec(kv_norm_w, layer), _layer_spec(w['wk'], layer), _layer_spec(wv, layer),
        ] + [pl.BlockSpec((layer, tm, x.shape[-1]), stacked) for x in prev],
        out_specs=out_specs,
        out_shape=out_shape,
        compiler_params=_params("arbitrary", "arbitrary"),
        name="proj",
    )(h, tab, norm_w, w['win'], q_norm_w, w['wq'], kv_norm_w, w['wk'], wv, *prev)


def _visible_cap(t):
    kc = np.arange(t)[:, None] // CHUNK
    qc = np.arange(t)[None, :] // CHUNK
    return jnp.asarray(np.where(kc <= qc, np.finfo(np.float32).max, NEG), F32)


def _attn_prompt_kernel(q_ref, k_ref, vt_ref, cap_ref, o_ref, s_ref, acc_ref, m_ref, *, t, n_tiles):
    acc_ref[...] = jnp.zeros(acc_ref.shape, F32)
    m_ref[...] = jnp.full(m_ref.shape, NEG, F32)

    def produce(item, slot, masked):
        qi, kj = item
        q0 = pl.multiple_of(qi * t, t)
        k0 = pl.multiple_of(kj * t, t)
        out = []
        for hh in range(2):
            sl = slice(hh * LANES, (hh + 1) * LANES)
            s = _dot_nt(k_ref[pl.ds(k0, t), sl], q_ref[pl.ds(q0, t), sl])
            if masked:
                cols = []
                for lt in range(t // LANES):
                    r0 = (lt * LANES // CHUNK + 1) * CHUNK
                    col = s[:, lt * LANES:(lt + 1) * LANES]
                    capped = jnp.minimum(col[r0:], cap_ref[r0:, lt * LANES:(lt + 1) * LANES])
                    cols.append(jnp.concatenate([col[:r0], capped], axis=0))
                s = jnp.concatenate(cols, axis=1)
            s_ref[slot, hh] = s
            out.append(jnp.max(s, axis=0, keepdims=True))
        return tuple(out)

    def consume(smax, slot, item):
        qi, kj = item
        for hh in range(2):
            m = m_ref[qi, hh]
            m_new = jnp.maximum(m, smax[hh])
            p = jnp.exp2(s_ref[slot, hh] - m_new).astype(BF16)
            acc_ref[qi, hh] = jnp.exp2(m - m_new) * acc_ref[qi, hh] + _dot(vt_ref[kj, hh], p)
            m_ref[qi, hh] = m_new

    def run(first, count, advance, masked):
        if count == 0:
            return

        def step(carry, slot):
            smax, prev, cur = carry
            nxt = produce(cur, 1 - slot, masked)
            consume(smax, slot, prev)
            return nxt, cur, advance(cur)

        def steps(r, carry):
            for u in range(STEPS_PER_ITER):
                carry = step(carry, u % 2)
            return carry

        carry = lax.fori_loop(0, (count - 1) // STEPS_PER_ITER, steps,
                              (produce(first, 0, masked), first, advance(first)))
        left = (count - 1) % STEPS_PER_ITER
        for u in range(left):
            carry = step(carry, u % 2)
        consume(carry[0], left % 2, carry[1])

    def next_below(item):
        qi, kj = item
        last = kj == qi - 1
        return jnp.where(last, qi + 1, qi), jnp.where(last, 0, kj + 1)

    run((jnp.int32(1), jnp.int32(0)), n_tiles * (n_tiles - 1) // 2, next_below, False)
    run((jnp.int32(0), jnp.int32(0)), n_tiles, lambda item: (item[0] + 1, item[1] + 1), True)

    def finish(qi, carry):
        heads = [acc_ref[qi, hh, :MLA_V] * (1.0 / acc_ref[qi, hh, MLA_V:MLA_V + 1]) for hh in range(2)]
        o_ref[pl.ds(pl.multiple_of(qi * t, t), t), :] = jnp.concatenate(heads, axis=0).T.astype(BF16)
        return carry

    lax.fori_loop(0, n_tiles, finish, 0)


def _attn_prompt(q, k, vt, *, t):
    B, S, _ = q.shape
    n_tiles = S // t
    return pl.pallas_call(
        functools.partial(_attn_prompt_kernel, t=t, n_tiles=n_tiles),
        grid=(B, HEAD_PAIRS),
        in_specs=[pl.BlockSpec((None, S, 2 * LANES), lambda b, p: (b, 0, p)),
                  pl.BlockSpec((None, S, 2 * LANES), lambda b, p: (b, 0, p)),
                  pl.BlockSpec((None, n_tiles, 2, V_ROWS, t), lambda b, p: (b, 0, p, 0, 0)),
                  _const_spec((t, t))],
        out_specs=pl.BlockSpec((None, S, LANES), lambda b, p: (b, 0, p)),
        out_shape=jax.ShapeDtypeStruct((B, S, MLA_HEADS * MLA_V), BF16),
        scratch_shapes=[pltpu.VMEM((2, 2, t, t), F32), pltpu.VMEM((n_tiles, 2, V_ROWS, t), F32),
                        pltpu.VMEM((n_tiles, 2, 1, t), F32)],
        compiler_params=_params("arbitrary", "arbitrary"),
        name="attn_prompt",
    )(q, k, vt, _visible_cap(t))


def _attn_sample_kernel(q_ref, cn_ref, rn_ref, cc_ref, rc_ref, wk_ref, wv_ref, o_ref):
    L = q_ref.shape[0]
    q_lat, q_rot = [], []
    for h in range(MLA_HEADS):
        qh = q_ref[:, h * LANES:(h + 1) * LANES]
        q_lat.append(_dot_nt(qh, wk_ref[:, h * LANES:(h + 1) * LANES]))
        q_rot.append(qh[:, MLA_NOPE:MLA_NOPE + MLA_ROPE])
    q_lat = jnp.concatenate(q_lat, axis=0).astype(BF16)
    q_rot = jnp.concatenate(q_rot, axis=0)
    cc, rc = cc_ref[...].astype(BF16), rc_ref[...].astype(BF16)
    cn, rn = cn_ref[...].astype(BF16), rn_ref[...].astype(BF16)
    sc = _dot_nt(q_lat, cc) + _dot_nt(q_rot, rc)
    sn = _dot_nt(q_lat, cn) + _dot_nt(q_rot, rn)
    m = jnp.maximum(jnp.max(sc, axis=-1, keepdims=True), jnp.max(sn, axis=-1, keepdims=True))
    pc, pn = jnp.exp2(sc - m), jnp.exp2(sn - m)
    l = jnp.sum(pc, axis=-1, keepdims=True) + jnp.sum(pn, axis=-1, keepdims=True)
    o_lat = ((_dot(pc.astype(BF16), cc) + _dot(pn.astype(BF16), cn)) / l).astype(BF16)
    lane = lax.broadcasted_iota(jnp.int32, (L, LANES), 1)
    for p in range(HEAD_PAIRS):
        wvp = wv_ref[:, p * LANES:(p + 1) * LANES]
        even = _dot(o_lat[2 * p * L:(2 * p + 1) * L], wvp)
        odd = _dot(o_lat[(2 * p + 1) * L:(2 * p + 2) * L], wvp)
        o_ref[:, p * LANES:(p + 1) * LANES] = jnp.where(lane < MLA_V, even, odd).astype(BF16)


def _attn_sample(q, lat_new, cache_c, cache_r, w, layer, *, past):
    B, L, _ = q.shape
    new = lambda b: (layer, b, 0)
    old = lambda b: (layer * B + b, 0)
    return pl.pallas_call(
        _attn_sample_kernel,
        grid=(B,),
        in_specs=[pl.BlockSpec((None, L, MLA_HEADS * LANES), lambda b: (b, 0, 0)),
                  pl.BlockSpec((None, L, MLA_KV_LORA), new), pl.BlockSpec((None, L, MLA_ROPE), new),
                  pl.BlockSpec((past, MLA_KV_LORA), old), pl.BlockSpec((past, MLA_ROPE), old),
                  _layer_spec(w['wk'], layer), _layer_spec(w['wv'], layer)],
        out_specs=pl.BlockSpec((None, L, MLA_HEADS * MLA_V), lambda b: (b, 0, 0)),
        out_shape=jax.ShapeDtypeStruct((B, L, MLA_HEADS * MLA_V), BF16),
        compiler_params=_params("arbitrary"),
        name="attn_sample",
    )(q, *lat_new, cache_c, cache_r, w['wk'], w['wv'])


def _ret_kernel(rq_ref, rk_ref, rv_ref, t0_ref, dec_ref, qd_ref, kd_ref, rs_ref, o_ref, tn_ref, st_ref):
    L = rq_ref.shape[0]
    blk = pl.program_id(1)

    def head_block(h):
        r0, c0 = (h % 4) * RET_DK, (h % 2) * RET_DV
        return h // 2, slice(r0, r0 + RET_DK), slice(c0, c0 + RET_DV)

    @pl.when(blk == 0)
    def _():
        st_ref[...] = jnp.zeros(st_ref.shape, F32)
        for h in range(RET_HEADS):
            st_ref[head_block(h)] = t0_ref[h]

    lane = lax.broadcasted_iota(jnp.int32, (L, LANES), 1)
    low = lane < RET_DV
    srow = lax.broadcasted_iota(jnp.int32, (LANES, LANES), 0) // RET_DK
    scol = lax.broadcasted_iota(jnp.int32, (LANES, LANES), 1) // RET_DV
    qk = [(rq_ref[:, s * LANES:(s + 1) * LANES], rk_ref[:, s * LANES:(s + 1) * LANES]) for s in range(2)]
    vps = [rv_ref[:, p * LANES:(p + 1) * LANES] for p in range(HEAD_PAIRS)]
    states = [st_ref[p] for p in range(HEAD_PAIRS)]

    cross = [_dot((qk[p // 2][0].astype(F32) * qd_ref[p // 2]).astype(BF16), states[p].astype(BF16))
             for p in range(HEAD_PAIRS)]
    raw = []
    for h in range(RET_HEADS):
        qs, ks = qk[h // 4]
        raw.append(_dot_nt(jnp.where(lane // RET_DK == h % 4, qs, jnp.zeros_like(qs)), ks))
    for p in range(HEAD_PAIRS):
        ks = qk[p // 2][1]
        kdec = (ks.astype(F32) * kd_ref[p // 2]).astype(BF16)
        new_state = rs_ref[p // 2] * states[p] + _dot_tn(kdec, vps[p])
        st_ref[p] = jnp.where(srow == 2 * (p % 2) + scol, new_state, 0.0)

    for p in range(HEAD_PAIRS):
        o = cross[p]
        for c in range(2):
            h = 2 * p + c
            oh = _dot((raw[h] * dec_ref[h]).astype(BF16), vps[p])
            o = o + jnp.where(low if c == 0 else ~low, oh, 0.0)
        o_ref[:, p * LANES:(p + 1) * LANES] = o.astype(BF16)

    @pl.when(blk == pl.num_programs(1) - 1)
    def _():
        for h in range(RET_HEADS):
            tn_ref[h] = st_ref[head_block(h)]


def _retention(rq, rk, rv, t0, *, L):
    B, S, _ = rq.shape
    dec, qd, kd, rs = _retention_tables(L)
    blk = lambda b, i: (b, i, 0)
    st = lambda b, i: (b, 0, 0, 0)
    state_spec = pl.BlockSpec((None, RET_HEADS, RET_DK, RET_DV), st)
    return pl.pallas_call(
        _ret_kernel,
        grid=(B, S // L),
        in_specs=[pl.BlockSpec((None, L, 256), blk), pl.BlockSpec((None, L, 256), blk),
                  pl.BlockSpec((None, L, 512), blk), state_spec,
                  _const_spec(dec.shape), _const_spec(qd.shape), _const_spec(kd.shape), _const_spec(rs.shape)],
        out_specs=[pl.BlockSpec((None, L, 512), blk), state_spec],
        out_shape=[jax.ShapeDtypeStruct((B, S, 512), BF16),
                   jax.ShapeDtypeStruct((B, RET_HEADS, RET_DK, RET_DV), t0.dtype)],
        scratch_shapes=[pltpu.VMEM((HEAD_PAIRS, LANES, LANES), F32)],
        compiler_params=_params("arbitrary", "arbitrary"),
        name="retention",
    )(rq, rk, rv, t0, dec, qd, kd, rs)


def _post_kernel(h_ref, att_ref, ret_ref, g_ref, p_ref, gnw_ref, woa_ref, wor_ref, nf_ref, w1_ref, w2_ref,
                 np_ref, wg_ref, wp_ref, fin_ref, o_ref, *, final, fc):
    ret = []
    for p in range(HEAD_PAIRS):
        sl = slice(p * LANES, (p + 1) * LANES)
        on = _group_norm_pair(ret_ref[:, sl].astype(F32), gnw_ref[:, sl])
        ret.append((g_ref[:, sl].astype(F32) * on).astype(BF16))
    ret = jnp.concatenate(ret, axis=1)
    h = h_ref[...] + _dot(att_ref[...], woa_ref[...]) + _dot(ret, wor_ref[...])
    a = _rms(h, nf_ref[...]).astype(BF16)
    acc = jnp.zeros_like(h)
    for c in range(D_FF // fc):
        u = _dot(a, w1_ref[:, c * fc:(c + 1) * fc])
        acc = acc + _dot(jnp.square(jnp.maximum(u, 0.0)).astype(BF16), w2_ref[c * fc:(c + 1) * fc, :])
    h = h + acc
    gate = jax.nn.sigmoid(_dot(_rms(h, np_ref[...]).astype(BF16), wg_ref[...]))
    h = h + _dot(p_ref[...].astype(BF16), wp_ref[...]) * gate
    o_ref[...] = _rms(h, fin_ref[...]) if final else h


def _post(h, att, ret, g, p, gn_w, norm_ffn_w, norm_ple_w, final_norm_w, w, layer, *, tm, final):
    N = h.shape[0]
    row = lambda i: (i, 0)
    return pl.pallas_call(
        functools.partial(_post_kernel, final=final, fc=512),
        grid=(N // tm,),
        in_specs=[pl.BlockSpec((tm, D_MODEL), row), pl.BlockSpec((tm, 512), row),
                  pl.BlockSpec((tm, 512), row), pl.BlockSpec((tm, 512), row),
                  pl.BlockSpec((tm, PLE_DIM), lambda i: (layer * (N // tm) + i, 0)),
                  _layer_spec(gn_w, layer), _layer_spec(w['woa'], layer), _layer_spec(w['wor'], layer), _layer_spec(norm_ffn_w, layer),
                  _layer_spec(w['w1'], layer), _layer_spec(w['w2'], layer), _layer_spec(norm_ple_w, layer),
                  _layer_spec(w['wg'], layer), _layer_spec(w['wp'], layer), _const_spec((1, D_MODEL))],
        out_specs=pl.BlockSpec((tm, D_MODEL), row),
        out_shape=jax.ShapeDtypeStruct((N, D_MODEL), F32),
        compiler_params=_params("arbitrary"),
        name="post",
    )(h, att, ret, g, p, gn_w, w['woa'], w['wor'], norm_ffn_w, w['w1'], w['w2'], norm_ple_w, w['wg'],
      w['wp'], final_norm_w.reshape(1, -1))


PROMPT_TM = 512
RET_BLOCK = 256
STEPS_PER_ITER = 8


def kernel(x_prompt, x_sample, cache_ckv, cache_krope, state_ret, p_prompt, p_sample, norm_mix_w, w_in,
           q_norm_w, w_uq, kv_norm_w, w_ukv, ret_gn_w, w_out, norm_ffn_w, w_ff1, w_ff2, norm_ple_w,
           w_ple_gate, w_ple_proj, final_norm_w):
    B, S, D = x_prompt.shape
    Bd, Ld, _ = x_sample.shape
    depth, _, past, _ = cache_ckv.shape
    tm_p = min(PROMPT_TM, S)
    assert past % CHUNK == 0 and Ld <= CHUNK and S % tm_p == 0 and tm_p % CHUNK == 0
    Np, Ns = B * S, Bd * Ld

    w = _prep_weights(w_in, w_uq, w_ukv, w_out, w_ff1, w_ff2, w_ple_gate, w_ple_proj)
    norm_mix_w, q_norm_w, kv_norm_w, ret_gn_w, norm_ffn_w, norm_ple_w = map(
        _rows, (norm_mix_w, q_norm_w, kv_norm_w, ret_gn_w, norm_ffn_w, norm_ple_w))
    tab_p = _rope_table(jnp.arange(S, dtype=jnp.int32))
    tab_s = jnp.tile(_rope_table(past + jnp.arange(Ld, dtype=jnp.int32)), (1, Bd, 1))
    zero_state = jnp.zeros((B, RET_HEADS, RET_DK, RET_DV), x_prompt.dtype)
    pp = p_prompt.reshape(depth * Np, PLE_DIM)
    ps = p_sample.reshape(depth * Ns, PLE_DIM)
    cache_c = cache_ckv.reshape(depth * Bd * past, MLA_KV_LORA)
    cache_r = cache_krope.reshape(depth * Bd * past, MLA_ROPE)

    hp = x_prompt.reshape(Np, D)
    hs = x_sample.reshape(Ns, D)
    lat_p = lat_s = None
    st_p, st_s = [], []
    for i in range(depth):
        final = i == depth - 1

        q, k, v, ckv, kr, rq, rk, rv, rg = _proj(hp, tab_p, norm_mix_w, q_norm_w, kv_norm_w, w, i, lat_p,
                                                 tm=tm_p, n_pos=S // tm_p, n_rep=B, v_rows=True)
        lat_p = (ckv, kr)
        att = _attn_prompt(q.reshape(B, S, -1), k.reshape(B, S, -1),
                           v.reshape(B, S // tm_p, MLA_HEADS, V_ROWS, tm_p), t=tm_p)
        ret, st = _retention(rq.reshape(B, S, -1), rk.reshape(B, S, -1), rv.reshape(B, S, -1), zero_state,
                             L=min(RET_BLOCK, S))
        hp = _post(hp, att.reshape(Np, -1), ret.reshape(Np, -1), rg, pp, ret_gn_w, norm_ffn_w, norm_ple_w,
                   final_norm_w, w, i, tm=tm_p, final=final)
        st_p.append(st)

        q, k, v, ckv, kr, rq, rk, rv, rg = _proj(hs, tab_s, norm_mix_w, q_norm_w, kv_norm_w, w, i, lat_s,
                                                 tm=Ns, n_pos=1, n_rep=1, v_rows=False)
        lat_s = (ckv, kr)
        att = _attn_sample(q.reshape(Bd, Ld, -1), lat_s, cache_c, cache_r, w, i, past=past)
        ret, st = _retention(rq.reshape(Bd, Ld, -1), rk.reshape(Bd, Ld, -1), rv.reshape(Bd, Ld, -1),
                             state_ret[i], L=Ld)
        hs = _post(hs, att.reshape(Ns, -1), ret.reshape(Ns, -1), rg, ps, ret_gn_w, norm_ffn_w, norm_ple_w,
                   final_norm_w, w, i, tm=Ns, final=final)
        st_s.append(st)

    return (hp.reshape(B, S, D), hs.reshape(Bd, Ld, D),
            lat_p[0].reshape(depth, B, S, -1), lat_p[1].reshape(depth, B, S, -1), jnp.stack(st_p),
            lat_s[0].reshape(depth, Bd, Ld, -1), lat_s[1].reshape(depth, Bd, Ld, -1), jnp.stack(st_s))
```

```python
import functools

import numpy as np
import jax
import jax.numpy as jnp
from jax import lax
from jax.experimental import pallas as pl
from jax.experimental.pallas import tpu as pltpu

D_MODEL = 1024
CHUNK = 64
PLE_DIM = 256
D_FF = 4 * D_MODEL
MLA_HEADS = 8
MLA_NOPE = 64
MLA_ROPE = 32
MLA_V = 64
MLA_Q_LORA = 384
MLA_KV_LORA = 256
MLA_SCALE = (MLA_NOPE + MLA_ROPE) ** -0.5
Q_SCALE = MLA_SCALE * float(np.log2(np.e))
RET_HEADS = 8
RET_DK = 32
RET_DV = 64
ROPE_THETA = 10000.0
EPS = 1e-6
NEG = -1e30
IN_SPLITS = (MLA_Q_LORA, MLA_KV_LORA, MLA_ROPE, RET_HEADS * RET_DK, RET_HEADS * RET_DK,
             RET_HEADS * RET_DV, RET_HEADS * RET_DV)

LANES = 128
HEAD_PAIRS = MLA_HEADS // 2
V_ROWS = MLA_V + 16
VMEM_LIMIT = 56 * 1024 * 1024

_C_Q, _C_C, _C_R, _C_RV, _C_RG, _C_KR, _C_END = 0, 384, 640, 1152, 1664, 2176, 2304

F32 = jnp.float32
BF16 = jnp.bfloat16


def _pad_cols(w, left, width):
    lead, n = w.shape[:-1], w.shape[-1]
    return jnp.concatenate([jnp.zeros((*lead, left), w.dtype), w,
                            jnp.zeros((*lead, width - left - n), w.dtype)], axis=-1)


def _prep_weights(w_in, w_uq, w_ukv, w_out, w_ff1, w_ff2, w_ple_gate, w_ple_proj):
    depth = w_in.shape[0]
    offs = np.cumsum((0,) + IN_SPLITS)
    q_lat, ckv, kr, rq, rk, rv, rg = [w_in[..., offs[n]:offs[n + 1]] for n in range(len(IN_SPLITS))]
    win = jnp.concatenate([q_lat, ckv, rq, rk, rv, rg, _pad_cols(kr, MLA_NOPE, LANES)], axis=-1)
    assert win.shape[-1] == _C_END
    uq = w_uq.reshape(depth, MLA_Q_LORA, MLA_HEADS, MLA_NOPE + MLA_ROPE)
    wq = _pad_cols(uq, 0, LANES).reshape(depth, MLA_Q_LORA, MLA_HEADS * LANES)
    ukv = w_ukv.reshape(depth, MLA_KV_LORA, MLA_HEADS, MLA_NOPE + MLA_V)
    wv = ukv[..., MLA_NOPE:].reshape(depth, MLA_KV_LORA, MLA_HEADS * MLA_V)
    return dict(
        win=win.astype(BF16),
        wq=wq.astype(BF16),
        wk=_pad_cols(ukv[..., :MLA_NOPE], 0, LANES).reshape(depth, MLA_KV_LORA, MLA_HEADS * LANES).astype(BF16),
        wv=wv.astype(BF16),
        wvt=jnp.swapaxes(wv, 1, 2).astype(BF16),
        woa=w_out[:, :MLA_HEADS * MLA_V].astype(BF16),
        wor=w_out[:, MLA_HEADS * MLA_V:].astype(BF16),
        w1=w_ff1.astype(BF16),
        w2=w_ff2.astype(BF16),
        wg=w_ple_gate.astype(BF16),
        wp=w_ple_proj.astype(BF16),
    )


def _rope_table(pos):
    half = MLA_ROPE // 2
    inv = ROPE_THETA ** (-jnp.arange(half, dtype=F32) / half)
    ang = pos.astype(F32)[:, None] * inv[None, :]
    cos, sin = jnp.cos(ang), jnp.sin(ang)
    return jnp.stack([jnp.tile(cos, (1, LANES // half)), jnp.tile(sin, (1, LANES // half))])


def _retention_tables(L):
    log_gamma = np.log1p(-np.exp2(-5.0 - np.arange(RET_HEADS)))
    idx = np.arange(L, dtype=np.float64)
    diff = idx[:, None] - idx[None, :]
    dec = np.where(diff >= 0, np.exp(np.maximum(diff, 0.0)[None] * log_gamma[:, None, None]), 0.0)
    lg_lane = np.repeat(log_gamma, RET_DK).reshape(2, 1, LANES)
    qd = np.exp((idx + 1.0)[None, :, None] * lg_lane)
    kd = np.exp((L - 1.0 - idx)[None, :, None] * lg_lane)
    rs = np.broadcast_to(np.exp(L * lg_lane).reshape(2, LANES, 1), (2, LANES, LANES))
    return [jnp.asarray(x, F32) for x in (dec, qd, kd, rs)]


def _rms(x, w):
    return x * lax.rsqrt(jnp.mean(x * x, axis=-1, keepdims=True) + EPS) * w


def _group_norm_pair(o, gain):
    low = lax.broadcasted_iota(jnp.int32, o.shape, 1) < RET_DV

    def half_mean(x):
        lo = jnp.sum(jnp.where(low, x, 0.0), axis=-1, keepdims=True)
        hi = jnp.sum(jnp.where(low, 0.0, x), axis=-1, keepdims=True)
        return jnp.where(low, lo, hi) * (1.0 / RET_DV)

    d = o - half_mean(o)
    return d * lax.rsqrt(half_mean(d * d) + EPS) * gain


def _dot(a, b):
    return jnp.dot(a, b, preferred_element_type=F32)


def _dot_nt(a, b):
    return lax.dot_general(a, b, (((1,), (1,)), ((), ())), preferred_element_type=F32)


def _dot_tn(a, b):
    return lax.dot_general(a, b, (((0,), (0,)), ((), ())), preferred_element_type=F32)


def _const_spec(shape):
    zeros = (0,) * len(shape)
    return pl.BlockSpec(shape, lambda *_: zeros, pipeline_mode=pl.Buffered(1))


def _layer_spec(stacked, layer):
    zeros = (0,) * (stacked.ndim - 1)
    return pl.BlockSpec((None,) + stacked.shape[1:], lambda *_: (layer,) + zeros,
                        pipeline_mode=pl.Buffered(1))


def _rows(w):
    return w.reshape(w.shape[0], 1, -1)


def _params(*sem):
    return pltpu.CompilerParams(dimension_semantics=sem, vmem_limit_bytes=VMEM_LIMIT)


def _proj_kernel(h_ref, tab_ref, nw_ref, win_ref, qnw_ref, wq_ref, kvnw_ref, wk_ref, wv_ref,
                 *refs, v_rows, n_prev):
    if n_prev:
        ckv_prev_ref, kr_prev_ref, *refs = refs
    q_ref, k_ref, v_ref, ckv_ref, kr_ref, rq_ref, rk_ref, rv_ref, rg_ref = refs
    tm = h_ref.shape[0]
    cos, sin = tab_ref[0], tab_ref[1]
    lane = lax.broadcasted_iota(jnp.int32, (tm, LANES), 1)
    rope_lanes = (lane >= MLA_NOPE) & (lane < MLA_NOPE + MLA_ROPE)
    half = MLA_ROPE // 2
    first_half = lane % MLA_ROPE < half
    sin = jnp.where(first_half, -sin, sin)

    def swap_halves(x):
        return jnp.where(first_half, pltpu.roll(x, LANES - half, 1), pltpu.roll(x, half, 1))

    a = _rms(h_ref[...], nw_ref[...]).astype(BF16)

    zq = _dot(a, win_ref[:, _C_Q:_C_C])
    zc = _dot(a, win_ref[:, _C_C:_C_R])
    zr = _dot(a, win_ref[:, _C_R:_C_RV])
    zv = _dot(a, win_ref[:, _C_RV:_C_RG])
    g = _dot(a, win_ref[:, _C_RG:_C_KR])
    zkr = _dot(a, win_ref[:, _C_KR:_C_END])
    ql = _rms(zq, qnw_ref[...]).astype(BF16)
    c = _rms(zc, kvnw_ref[...])
    cb = c.astype(BF16)
    qz = _dot(ql, wq_ref[...])
    kn = _dot(cb, wk_ref[...])
    vals = _dot_nt(wv_ref[...], cb) if v_rows else _dot(cb, wv_ref[...])

    cq = jnp.where(lane < MLA_NOPE, Q_SCALE, jnp.where(rope_lanes, cos * Q_SCALE, 0.0))
    sq = jnp.where(rope_lanes, sin * Q_SCALE, 0.0)
    for h in range(MLA_HEADS):
        sl = slice(h * LANES, (h + 1) * LANES)
        q_ref[:, sl] = (qz[:, sl] * cq + swap_halves(qz[:, sl]) * sq).astype(BF16)

    if n_prev:
        ckv_ref[:n_prev] = ckv_prev_ref[...]
        kr_ref[:n_prev] = kr_prev_ref[...]
    ckv_ref[n_prev] = c
    if v_rows:
        ones_row = jnp.where(lax.broadcasted_iota(jnp.int32, (V_ROWS - MLA_V, tm), 0) == 0, 1.0, 0.0)
        for h in range(MLA_HEADS):
            v_ref[h, :MLA_V, :] = vals[h * MLA_V:(h + 1) * MLA_V, :].astype(BF16)
            v_ref[h, MLA_V:, :] = ones_row.astype(BF16)
    else:
        v_ref[...] = vals.astype(BF16)
    kslab = jnp.where(rope_lanes, zkr * cos + swap_halves(zkr) * sin, 0.0)
    kr_ref[n_prev] = kslab[:, MLA_NOPE:MLA_NOPE + MLA_ROPE]
    for h in range(MLA_HEADS):
        sl = slice(h * LANES, (h + 1) * LANES)
        k_ref[:, sl] = (kn[:, sl] + kslab).astype(BF16)

    kscale = RET_DK ** -0.5
    for s in range(2):
        sl = slice(s * LANES, (s + 1) * LANES)
        zq_s, zk_s = zr[:, sl], zr[:, 256 + s * LANES:256 + (s + 1) * LANES]
        rq_ref[:, sl] = (zq_s * cos + swap_halves(zq_s) * sin).astype(BF16)
        rk_ref[:, sl] = ((zk_s * cos + swap_halves(zk_s) * sin) * kscale).astype(BF16)
    rv_ref[...] = zv.astype(BF16)
    rg_ref[...] = (g * jax.nn.sigmoid(g)).astype(BF16)


def _proj(h, tab, norm_w, q_norm_w, kv_norm_w, w, layer, prev, *, tm, n_pos, n_rep, v_rows):
    N = h.shape[0]
    row = lambda s, b: (b * n_pos + s, 0)
    stacked = lambda s, b: (0, b * n_pos + s, 0)
    outs = [(MLA_HEADS * LANES, BF16), (MLA_HEADS * LANES, BF16), (MLA_HEADS * MLA_V, BF16),
            (MLA_KV_LORA, F32), (MLA_ROPE, F32), (256, BF16), (256, BF16), (512, BF16), (512, BF16)]
    out_specs = [pl.BlockSpec((tm, c), row) for c, _ in outs]
    out_shape = [jax.ShapeDtypeStruct((N, c), dt) for c, dt in outs]
    for n in (3, 4):
        out_specs[n] = pl.BlockSpec((layer + 1, tm, outs[n][0]), stacked)
        out_shape[n] = jax.ShapeDtypeStruct((layer + 1, N, outs[n][0]), F32)
    wv = w['wv']
    if v_rows:
        out_specs[2] = pl.BlockSpec((None, MLA_HEADS, V_ROWS, tm), lambda s, b: (b * n_pos + s, 0, 0, 0))
        out_shape[2] = jax.ShapeDtypeStruct((N // tm, MLA_HEADS, V_ROWS, tm), BF16)
        wv = w['wvt']
    prev = () if prev is None else prev
    assert len(prev) == (2 if layer else 0)
    return pl.pallas_call(
        functools.partial(_proj_kernel, v_rows=v_rows, n_prev=layer),
        grid=(n_pos, n_rep),
        in_specs=[
            pl.BlockSpec((tm, D_MODEL), row),
            pl.BlockSpec((2, tm, LANES), lambda s, b: (0, s, 0)),
            _layer_spec(norm_w, layer), _layer_spec(w['win'], layer),
            _layer_spec(q_norm_w, layer), _layer_spec(w['wq'], layer),
            _layer_spec(kv_norm_w, layer), _layer_spec(w['wk'], layer), _layer_spec(wv, layer),
        ] + [pl.BlockSpec((layer, tm, x.shape[-1]), stacked) for x in prev],
        out_specs=out_specs,
        out_shape=out_shape,
        compiler_params=_params("arbitrary", "arbitrary"),
        name="proj",
    )(h, tab, norm_w, w['win'], q_norm_w, w['wq'], kv_norm_w, w['wk'], wv, *prev)


def _visible_cap(t):
    kc = np.arange(t)[:, None] // CHUNK
    qc = np.arange(t)[None, :] // CHUNK
    return jnp.asarray(np.where(kc <= qc, np.finfo(np.float32).max, NEG), F32)


def _attn_prompt_kernel(q_ref, k_ref, vt_ref, cap_ref, o_ref, s_ref, acc_ref, m_ref, *, t, n_tiles):
    acc_ref[...] = jnp.zeros(acc_ref.shape, F32)
    m_ref[...] = jnp.full(m_ref.shape, NEG, F32)

    def produce(item, slot, masked):
        qi, kj = item
        q0 = pl.multiple_of(qi * t, t)
        k0 = pl.multiple_of(kj * t, t)
        out = []
        for hh in range(2):
            sl = slice(hh * LANES, (hh + 1) * LANES)
            s = _dot_nt(k_ref[pl.ds(k0, t), sl], q_ref[pl.ds(q0, t), sl])
            if masked:
                cols = []
                for lt in range(t // LANES):
                    r0 = (lt * LANES // CHUNK + 1) * CHUNK
                    col = s[:, lt * LANES:(lt + 1) * LANES]
                    capped = jnp.minimum(col[r0:], cap_ref[r0:, lt * LANES:(lt + 1) * LANES])
                    cols.append(jnp.concatenate([col[:r0], capped], axis=0))
                s = jnp.concatenate(cols, axis=1)
            s_ref[slot, hh] = s
            out.append(jnp.max(s, axis=0, keepdims=True))
        return tuple(out)

    def consume(smax, slot, item):
        qi, kj = item
        for hh in range(2):
            m = m_ref[qi, hh]
            m_new = jnp.maximum(m, smax[hh])
            p = jnp.exp2(s_ref[slot, hh] - m_new).astype(BF16)
            acc_ref[qi, hh] = jnp.exp2(m - m_new) * acc_ref[qi, hh] + _dot(vt_ref[kj, hh], p)
            m_ref[qi, hh] = m_new

    def run(first, count, advance, masked):
        if count == 0:
            return

        def step(carry, slot):
            smax, prev, cur = carry
            nxt = produce(cur, 1 - slot, masked)
            consume(smax, slot, prev)
            return nxt, cur, advance(cur)

        def steps(r, carry):
            for u in range(STEPS_PER_ITER):
                carry = step(carry, u % 2)
            return carry

        carry = lax.fori_loop(0, (count - 1) // STEPS_PER_ITER, steps,
                              (produce(first, 0, masked), first, advance(first)))
        left = (count - 1) % STEPS_PER_ITER
        for u in range(left):
            carry = step(carry, u % 2)
        consume(carry[0], left % 2, carry[1])

    def next_below(item):
        qi, kj = item
        last = kj == qi - 1
        return jnp.where(last, qi + 1, qi), jnp.where(last, 0, kj + 1)

    run((jnp.int32(1), jnp.int32(0)), n_tiles * (n_tiles - 1) // 2, next_below, False)
    run((jnp.int32(0), jnp.int32(0)), n_tiles, lambda item: (item[0] + 1, item[1] + 1), True)

    def finish(qi, carry):
        heads = [acc_ref[qi, hh, :MLA_V] * (1.0 / acc_ref[qi, hh, MLA_V:MLA_V + 1]) for hh in range(2)]
        o_ref[pl.ds(pl.multiple_of(qi * t, t), t), :] = jnp.concatenate(heads, axis=0).T.astype(BF16)
        return carry

    lax.fori_loop(0, n_tiles, finish, 0)


def _attn_prompt(q, k, vt, *, t):
    B, S, _ = q.shape
    n_tiles = S // t
    return pl.pallas_call(
        functools.partial(_attn_prompt_kernel, t=t, n_tiles=n_tiles),
        grid=(B, HEAD_PAIRS),
        in_specs=[pl.BlockSpec((None, S, 2 * LANES), lambda b, p: (b, 0, p)),
                  pl.BlockSpec((None, S, 2 * LANES), lambda b, p: (b, 0, p)),
                  pl.BlockSpec((None, n_tiles, 2, V_ROWS, t), lambda b, p: (b, 0, p, 0, 0)),
                  _const_spec((t, t))],
        out_specs=pl.BlockSpec((None, S, LANES), lambda b, p: (b, 0, p)),
        out_shape=jax.ShapeDtypeStruct((B, S, MLA_HEADS * MLA_V), BF16),
        scratch_shapes=[pltpu.VMEM((2, 2, t, t), F32), pltpu.VMEM((n_tiles, 2, V_ROWS, t), F32),
                        pltpu.VMEM((n_tiles, 2, 1, t), F32)],
        compiler_params=_params("arbitrary", "arbitrary"),
        name="attn_prompt",
    )(q, k, vt, _visible_cap(t))


def _attn_sample_kernel(q_ref, cn_ref, rn_ref, cc_ref, rc_ref, wk_ref, wv_ref, o_ref):
    L = q_ref.shape[0]
    q_lat, q_rot = [], []
    for h in range(MLA_HEADS):
        qh = q_ref[:, h * LANES:(h + 1) * LANES]
        q_lat.append(_dot_nt(qh, wk_ref[:, h * LANES:(h + 1) * LANES]))
        q_rot.append(qh[:, MLA_NOPE:MLA_NOPE + MLA_ROPE])
    q_lat = jnp.concatenate(q_lat, axis=0).astype(BF16)
    q_rot = jnp.concatenate(q_rot, axis=0)
    cc, rc = cc_ref[...].astype(BF16), rc_ref[...].astype(BF16)
    cn, rn = cn_ref[...].astype(BF16), rn_ref[...].astype(BF16)
    sc = _dot_nt(q_lat, cc) + _dot_nt(q_rot, rc)
    sn = _dot_nt(q_lat, cn) + _dot_nt(q_rot, rn)
    m = jnp.maximum(jnp.max(sc, axis=-1, keepdims=True), jnp.max(sn, axis=-1, keepdims=True))
    pc, pn = jnp.exp2(sc - m), jnp.exp2(sn - m)
    l = jnp.sum(pc, axis=-1, keepdims=True) + jnp.sum(pn, axis=-1, keepdims=True)
    o_lat = ((_dot(pc.astype(BF16), cc) + _dot(pn.astype(BF16), cn)) / l).astype(BF16)
    lane = lax.broadcasted_iota(jnp.int32, (L, LANES), 1)
    for p in range(HEAD_PAIRS):
        wvp = wv_ref[:, p * LANES:(p + 1) * LANES]
        even = _dot(o_lat[2 * p * L:(2 * p + 1) * L], wvp)
        odd = _dot(o_lat[(2 * p + 1) * L:(2 * p + 2) * L], wvp)
        o_ref[:, p * LANES:(p + 1) * LANES] = jnp.where(lane < MLA_V, even, odd).astype(BF16)


def _attn_sample(q, lat_new, cache_c, cache_r, w, layer, *, past):
    B, L, _ = q.shape
    new = lambda b: (layer, b, 0)
    old = lambda b: (layer * B + b, 0)
    return pl.pallas_call(
        _attn_sample_kernel,
        grid=(B,),
        in_specs=[pl.BlockSpec((None, L, MLA_HEADS * LANES), lambda b: (b, 0, 0)),
                  pl.BlockSpec((None, L, MLA_KV_LORA), new), pl.BlockSpec((None, L, MLA_ROPE), new),
                  pl.BlockSpec((past, MLA_KV_LORA), old), pl.BlockSpec((past, MLA_ROPE), old),
                  _layer_spec(w['wk'], layer), _layer_spec(w['wv'], layer)],
        out_specs=pl.BlockSpec((None, L, MLA_HEADS * MLA_V), lambda b: (b, 0, 0)),
        out_shape=jax.ShapeDtypeStruct((B, L, MLA_HEADS * MLA_V), BF16),
        compiler_params=_params("arbitrary"),
        name="attn_sample",
    )(q, *lat_new, cache_c, cache_r, w['wk'], w['wv'])


def _ret_kernel(rq_ref, rk_ref, rv_ref, t0_ref, dec_ref, qd_ref, kd_ref, rs_ref, o_ref, tn_ref, st_ref):
    L = rq_ref.shape[0]
    blk = pl.program_id(1)

    def head_block(h):
        r0, c0 = (h % 4) * RET_DK, (h % 2) * RET_DV
        return h // 2, slice(r0, r0 + RET_DK), slice(c0, c0 + RET_DV)

    @pl.when(blk == 0)
    def _():
        st_ref[...] = jnp.zeros(st_ref.shape, F32)
        for h in range(RET_HEADS):
            st_ref[head_block(h)] = t0_ref[h]

    lane = lax.broadcasted_iota(jnp.int32, (L, LANES), 1)
    low = lane < RET_DV
    srow = lax.broadcasted_iota(jnp.int32, (LANES, LANES), 0) // RET_DK
    scol = lax.broadcasted_iota(jnp.int32, (LANES, LANES), 1) // RET_DV
    qk = [(rq_ref[:, s * LANES:(s + 1) * LANES], rk_ref[:, s * LANES:(s + 1) * LANES]) for s in range(2)]
    vps = [rv_ref[:, p * LANES:(p + 1) * LANES] for p in range(HEAD_PAIRS)]
    states = [st_ref[p] for p in range(HEAD_PAIRS)]

    cross = [_dot((qk[p // 2][0].astype(F32) * qd_ref[p // 2]).astype(BF16), states[p].astype(BF16))
             for p in range(HEAD_PAIRS)]
    raw = []
    for h in range(RET_HEADS):
        qs, ks = qk[h // 4]
        raw.append(_dot_nt(jnp.where(lane // RET_DK == h % 4, qs, jnp.zeros_like(qs)), ks))
    for p in range(HEAD_PAIRS):
        ks = qk[p // 2][1]
        kdec = (ks.astype(F32) * kd_ref[p // 2]).astype(BF16)
        new_state = rs_ref[p // 2] * states[p] + _dot_tn(kdec, vps[p])
        st_ref[p] = jnp.where(srow == 2 * (p % 2) + scol, new_state, 0.0)

    for p in range(HEAD_PAIRS):
        o = cross[p]
        for c in range(2):
            h = 2 * p + c
            oh = _dot((raw[h] * dec_ref[h]).astype(BF16), vps[p])
            o = o + jnp.where(low if c == 0 else ~low, oh, 0.0)
        o_ref[:, p * LANES:(p + 1) * LANES] = o.astype(BF16)

    @pl.when(blk == pl.num_programs(1) - 1)
    def _():
        for h in range(RET_HEADS):
            tn_ref[h] = st_ref[head_block(h)]


def _retention(rq, rk, rv, t0, *, L):
    B, S, _ = rq.shape
    dec, qd, kd, rs = _retention_tables(L)
    blk = lambda b, i: (b, i, 0)
    st = lambda b, i: (b, 0, 0, 0)
    state_spec = pl.BlockSpec((None, RET_HEADS, RET_DK, RET_DV), st)
    return pl.pallas_call(
        _ret_kernel,
        grid=(B, S // L),
        in_specs=[pl.BlockSpec((None, L, 256), blk), pl.BlockSpec((None, L, 256), blk),
                  pl.BlockSpec((None, L, 512), blk), state_spec,
                  _const_spec(dec.shape), _const_spec(qd.shape), _const_spec(kd.shape), _const_spec(rs.shape)],
        out_specs=[pl.BlockSpec((None, L, 512), blk), state_spec],
        out_shape=[jax.ShapeDtypeStruct((B, S, 512), BF16),
                   jax.ShapeDtypeStruct((B, RET_HEADS, RET_DK, RET_DV), t0.dtype)],
        scratch_shapes=[pltpu.VMEM((HEAD_PAIRS, LANES, LANES), F32)],
        compiler_params=_params("arbitrary", "arbitrary"),
        name="retention",
    )(rq, rk, rv, t0, dec, qd, kd, rs)


def _post_kernel(h_ref, att_ref, ret_ref, g_ref, p_ref, gnw_ref, woa_ref, wor_ref, nf_ref, w1_ref, w2_ref,
                 np_ref, wg_ref, wp_ref, fin_ref, o_ref, *, final, fc):
    ret = []
    for p in range(HEAD_PAIRS):
        sl = slice(p * LANES, (p + 1) * LANES)
        on = _group_norm_pair(ret_ref[:, sl].astype(F32), gnw_ref[:, sl])
        ret.append((g_ref[:, sl].astype(F32) * on).astype(BF16))
    ret = jnp.concatenate(ret, axis=1)
    h = h_ref[...] + _dot(att_ref[...], woa_ref[...]) + _dot(ret, wor_ref[...])
    a = _rms(h, nf_ref[...]).astype(BF16)
    acc = jnp.zeros_like(h)
    for c in range(D_FF // fc):
        u = _dot(a, w1_ref[:, c * fc:(c + 1) * fc])
        acc = acc + _dot(jnp.square(jnp.maximum(u, 0.0)).astype(BF16), w2_ref[c * fc:(c + 1) * fc, :])
    h = h + acc
    gate = jax.nn.sigmoid(_dot(_rms(h, np_ref[...]).astype(BF16), wg_ref[...]))
    h = h + _dot(p_ref[...].astype(BF16), wp_ref[...]) * gate
    o_ref[...] = _rms(h, fin_ref[...]) if final else h


def _post(h, att, ret, g, p, gn_w, norm_ffn_w, norm_ple_w, final_norm_w, w, layer, *, tm, final):
    N = h.shape[0]
    row = lambda i: (i, 0)
    return pl.pallas_call(
        functools.partial(_post_kernel, final=final, fc=512),
        grid=(N // tm,),
        in_specs=[pl.BlockSpec((tm, D_MODEL), row), pl.BlockSpec((tm, 512), row),
                  pl.BlockSpec((tm, 512), row), pl.BlockSpec((tm, 512), row),
                  pl.BlockSpec((tm, PLE_DIM), lambda i: (layer * (N // tm) + i, 0)),
                  _layer_spec(gn_w, layer), _layer_spec(w['woa'], layer), _layer_spec(w['wor'], layer), _layer_spec(norm_ffn_w, layer),
                  _layer_spec(w['w1'], layer), _layer_spec(w['w2'], layer), _layer_spec(norm_ple_w, layer),
                  _layer_spec(w['wg'], layer), _layer_spec(w['wp'], layer), _const_spec((1, D_MODEL))],
        out_specs=pl.BlockSpec((tm, D_MODEL), row),
        out_shape=jax.ShapeDtypeStruct((N, D_MODEL), F32),
        compiler_params=_params("arbitrary"),
        name="post",
    )(h, att, ret, g, p, gn_w, w['woa'], w['wor'], norm_ffn_w, w['w1'], w['w2'], norm_ple_w, w['wg'],
      w['wp'], final_norm_w.reshape(1, -1))


PROMPT_TM = 512
RET_BLOCK = 256
STEPS_PER_ITER = 8


def kernel(x_prompt, x_sample, cache_ckv, cache_krope, state_ret, p_prompt, p_sample, norm_mix_w, w_in,
           q_norm_w, w_uq, kv_norm_w, w_ukv, ret_gn_w, w_out, norm_ffn_w, w_ff1, w_ff2, norm_ple_w,
           w_ple_gate, w_ple_proj, final_norm_w):
    B, S, D = x_prompt.shape
    Bd, Ld, _ = x_sample.shape
    depth, _, past, _ = cache_ckv.shape
    tm_p = min(PROMPT_TM, S)
    assert past % CHUNK == 0 and Ld <= CHUNK and S % tm_p == 0 and tm_p % CHUNK == 0
    Np, Ns = B * S, Bd * Ld

    w = _prep_weights(w_in, w_uq, w_ukv, w_out, w_ff1, w_ff2, w_ple_gate, w_ple_proj)
    norm_mix_w, q_norm_w, kv_norm_w, ret_gn_w, norm_ffn_w, norm_ple_w = map(
        _rows, (norm_mix_w, q_norm_w, kv_norm_w, ret_gn_w, norm_ffn_w, norm_ple_w))
    tab_p = _rope_table(jnp.arange(S, dtype=jnp.int32))
    tab_s = jnp.tile(_rope_table(past + jnp.arange(Ld, dtype=jnp.int32)), (1, Bd, 1))
    zero_state = jnp.zeros((B, RET_HEADS, RET_DK, RET_DV), x_prompt.dtype)
    pp = p_prompt.reshape(depth * Np, PLE_DIM)
    ps = p_sample.reshape(depth * Ns, PLE_DIM)
    cache_c = cache_ckv.reshape(depth * Bd * past, MLA_KV_LORA)
    cache_r = cache_krope.reshape(depth * Bd * past, MLA_ROPE)

    hp = x_prompt.reshape(Np, D)
    hs = x_sample.reshape(Ns, D)
    lat_p = lat_s = None
    st_p, st_s = [], []
    for i in range(depth):
        final = i == depth - 1

        q, k, v, ckv, kr, rq, rk, rv, rg = _proj(hp, tab_p, norm_mix_w, q_norm_w, kv_norm_w, w, i, lat_p,
                                                 tm=tm_p, n_pos=S // tm_p, n_rep=B, v_rows=True)
        lat_p = (ckv, kr)
        att = _attn_prompt(q.reshape(B, S, -1), k.reshape(B, S, -1),
                           v.reshape(B, S // tm_p, MLA_HEADS, V_ROWS, tm_p), t=tm_p)
        ret, st = _retention(rq.reshape(B, S, -1), rk.reshape(B, S, -1), rv.reshape(B, S, -1), zero_state,
                             L=min(RET_BLOCK, S))
        hp = _post(hp, att.reshape(Np, -1), ret.reshape(Np, -1), rg, pp, ret_gn_w, norm_ffn_w, norm_ple_w,
                   final_norm_w, w, i, tm=tm_p, final=final)
        st_p.append(st)

        q, k, v, ckv, kr, rq, rk, rv, rg = _proj(hs, tab_s, norm_mix_w, q_norm_w, kv_norm_w, w, i, lat_s,
                                                 tm=Ns, n_pos=1, n_rep=1, v_rows=False)
        lat_s = (ckv, kr)
        att = _attn_sample(q.reshape(Bd, Ld, -1), lat_s, cache_c, cache_r, w, i, past=past)
        ret, st = _retention(rq.reshape(Bd, Ld, -1), rk.reshape(Bd, Ld, -1), rv.reshape(Bd, Ld, -1),
                             state_ret[i], L=Ld)
        hs = _post(hs, att.reshape(Ns, -1), ret.reshape(Ns, -1), rg, ps, ret_gn_w, norm_ffn_w, norm_ple_w,
                   final_norm_w, w, i, tm=Ns, final=final)
        st_s.append(st)

    return (hp.reshape(B, S, D), hs.reshape(Bd, Ld, D),
            lat_p[0].reshape(depth, B, S, -1), lat_p[1].reshape(depth, B, S, -1), jnp.stack(st_p),
            lat_s[0].reshape(depth, Bd, Ld, -1), lat_s[1].reshape(depth, Bd, Ld, -1), jnp.stack(st_s))
```

```python
import functools

import numpy as np
import jax
import jax.numpy as jnp
from jax import lax
from jax.experimental import pallas as pl
from jax.experimental.pallas import tpu as pltpu

D_MODEL = 1024
CHUNK = 64
PLE_DIM = 256
D_FF = 4 * D_MODEL
MLA_HEADS = 8
MLA_NOPE = 64
MLA_ROPE = 32
MLA_V = 64
MLA_Q_LORA = 384
MLA_KV_LORA = 256
MLA_SCALE = (MLA_NOPE + MLA_ROPE) ** -0.5
Q_SCALE = MLA_SCALE * float(np.log2(np.e))
RET_HEADS = 8
RET_DK = 32
RET_DV = 64
ROPE_THETA = 10000.0
EPS = 1e-6
NEG = -1e30
IN_SPLITS = (MLA_Q_LORA, MLA_KV_LORA, MLA_ROPE, RET_HEADS * RET_DK, RET_HEADS * RET_DK,
             RET_HEADS * RET_DV, RET_HEADS * RET_DV)

LANES = 128
HEAD_PAIRS = MLA_HEADS // 2
V_ROWS = MLA_V + 16
VMEM_LIMIT = 56 * 1024 * 1024

_C_Q, _C_C, _C_R, _C_RV, _C_RG, _C_KR, _C_END = 0, 384, 640, 1152, 1664, 2176, 2304

F32 = jnp.float32
BF16 = jnp.bfloat16


def _pad_cols(w, left, width):
    lead, n = w.shape[:-1], w.shape[-1]
    return jnp.concatenate([jnp.zeros((*lead, left), w.dtype), w,
                            jnp.zeros((*lead, width - left - n), w.dtype)], axis=-1)


def _prep_weights(w_in, w_uq, w_ukv, w_out, w_ff1, w_ff2, w_ple_gate, w_ple_proj):
    depth = w_in.shape[0]
    offs = np.cumsum((0,) + IN_SPLITS)
    q_lat, ckv, kr, rq, rk, rv, rg = [w_in[..., offs[n]:offs[n + 1]] for n in range(len(IN_SPLITS))]
    win = jnp.concatenate([q_lat, ckv, rq, rk, rv, rg, _pad_cols(kr, MLA_NOPE, LANES)], axis=-1)
    assert win.shape[-1] == _C_END
    uq = w_uq.reshape(depth, MLA_Q_LORA, MLA_HEADS, MLA_NOPE + MLA_ROPE)
    wq = _pad_cols(uq, 0, LANES).reshape(depth, MLA_Q_LORA, MLA_HEADS * LANES)
    ukv = w_ukv.reshape(depth, MLA_KV_LORA, MLA_HEADS, MLA_NOPE + MLA_V)
    wv = ukv[..., MLA_NOPE:].reshape(depth, MLA_KV_LORA, MLA_HEADS * MLA_V)
    return dict(
        win=win.astype(BF16),
        wq=wq.astype(BF16),
        wk=_pad_cols(ukv[..., :MLA_NOPE], 0, LANES).reshape(depth, MLA_KV_LORA, MLA_HEADS * LANES).astype(BF16),
        wv=wv.astype(BF16),
        wvt=jnp.swapaxes(wv, 1, 2).astype(BF16),
        woa=w_out[:, :MLA_HEADS * MLA_V].astype(BF16),
        wor=w_out[:, MLA_HEADS * MLA_V:].astype(BF16),
        w1=w_ff1.astype(BF16),
        w2=w_ff2.astype(BF16),
        wg=w_ple_gate.astype(BF16),
        wp=w_ple_proj.astype(BF16),
    )


def _rope_table(pos):
    half = MLA_ROPE // 2
    inv = ROPE_THETA ** (-jnp.arange(half, dtype=F32) / half)
    ang = pos.astype(F32)[:, None] * inv[None, :]
    cos, sin = jnp.cos(ang), jnp.sin(ang)
    return jnp.stack([jnp.tile(cos, (1, LANES // half)), jnp.tile(sin, (1, LANES // half))])


def _retention_tables(L):
    log_gamma = np.log1p(-np.exp2(-5.0 - np.arange(RET_HEADS)))
    idx = np.arange(L, dtype=np.float64)
    diff = idx[:, None] - idx[None, :]
    dec = np.where(diff >= 0, np.exp(np.maximum(diff, 0.0)[None] * log_gamma[:, None, None]), 0.0)
    lg_lane = np.repeat(log_gamma, RET_DK).reshape(2, 1, LANES)
    qd = np.exp((idx + 1.0)[None, :, None] * lg_lane)
    kd = np.exp((L - 1.0 - idx)[None, :, None] * lg_lane)
    rs = np.broadcast_to(np.exp(L * lg_lane).reshape(2, LANES, 1), (2, LANES, LANES))
    return [jnp.asarray(x, F32) for x in (dec, qd, kd, rs)]


def _rms(x, w):
    return x * lax.rsqrt(jnp.mean(x * x, axis=-1, keepdims=True) + EPS) * w


def _group_norm_pair(o, gain):
    low = lax.broadcasted_iota(jnp.int32, o.shape, 1) < RET_DV

    def half_mean(x):
        lo = jnp.sum(jnp.where(low, x, 0.0), axis=-1, keepdims=True)
        hi = jnp.sum(jnp.where(low, 0.0, x), axis=-1, keepdims=True)
        return jnp.where(low, lo, hi) * (1.0 / RET_DV)

    d = o - half_mean(o)
    return d * lax.rsqrt(half_mean(d * d) + EPS) * gain


def _dot(a, b):
    return jnp.dot(a, b, preferred_element_type=F32)


def _dot_nt(a, b):
    return lax.dot_general(a, b, (((1,), (1,)), ((), ())), preferred_element_type=F32)


def _dot_tn(a, b):
    return lax.dot_general(a, b, (((0,), (0,)), ((), ())), preferred_element_type=F32)


def _const_spec(shape):
    zeros = (0,) * len(shape)
    return pl.BlockSpec(shape, lambda *_: zeros, pipeline_mode=pl.Buffered(1))


def _layer_spec(stacked, layer):
    zeros = (0,) * (stacked.ndim - 1)
    return pl.BlockSpec((None,) + stacked.shape[1:], lambda *_: (layer,) + zeros,
                        pipeline_mode=pl.Buffered(1))


def _rows(w):
    return w.reshape(w.shape[0], 1, -1)


def _params(*sem):
    return pltpu.CompilerParams(dimension_semantics=sem, vmem_limit_bytes=VMEM_LIMIT)


def _proj_kernel(h_ref, tab_ref, nw_ref, win_ref, qnw_ref, wq_ref, kvnw_ref, wk_ref, wv_ref,
                 *refs, v_rows, n_prev):
    if n_prev:
        ckv_prev_ref, kr_prev_ref, *refs = refs
    q_ref, k_ref, v_ref, ckv_ref, kr_ref, rq_ref, rk_ref, rv_ref, rg_ref = refs
    tm = h_ref.shape[0]
    cos, sin = tab_ref[0], tab_ref[1]
    lane = lax.broadcasted_iota(jnp.int32, (tm, LANES), 1)
    rope_lanes = (lane >= MLA_NOPE) & (lane < MLA_NOPE + MLA_ROPE)
    half = MLA_ROPE // 2
    first_half = lane % MLA_ROPE < half
    sin = jnp.where(first_half, -sin, sin)

    def swap_halves(x):
        return jnp.where(first_half, pltpu.roll(x, LANES - half, 1), pltpu.roll(x, half, 1))

    a = _rms(h_ref[...], nw_ref[...]).astype(BF16)

    zq = _dot(a, win_ref[:, _C_Q:_C_C])
    zc = _dot(a, win_ref[:, _C_C:_C_R])
    zr = _dot(a, win_ref[:, _C_R:_C_RV])
    zv = _dot(a, win_ref[:, _C_RV:_C_RG])
    g = _dot(a, win_ref[:, _C_RG:_C_KR])
    zkr = _dot(a, win_ref[:, _C_KR:_C_END])
    ql = _rms(zq, qnw_ref[...]).astype(BF16)
    c = _rms(zc, kvnw_ref[...])
    cb = c.astype(BF16)
    qz = _dot(ql, wq_ref[...])
    kn = _dot(cb, wk_ref[...])
    vals = _dot_nt(wv_ref[...], cb) if v_rows else _dot(cb, wv_ref[...])

    cq = jnp.where(lane < MLA_NOPE, Q_SCALE, jnp.where(rope_lanes, cos * Q_SCALE, 0.0))
    sq = jnp.where(rope_lanes, sin * Q_SCALE, 0.0)
    for h in range(MLA_HEADS):
        sl = slice(h * LANES, (h + 1) * LANES)
        q_ref[:, sl] = (qz[:, sl] * cq + swap_halves(qz[:, sl]) * sq).astype(BF16)

    if n_prev:
        ckv_ref[:n_prev] = ckv_prev_ref[...]
        kr_ref[:n_prev] = kr_prev_ref[...]
    ckv_ref[n_prev] = c
    if v_rows:
        ones_row = jnp.where(lax.broadcasted_iota(jnp.int32, (V_ROWS - MLA_V, tm), 0) == 0, 1.0, 0.0)
        for h in range(MLA_HEADS):
            v_ref[h, :MLA_V, :] = vals[h * MLA_V:(h + 1) * MLA_V, :].astype(BF16)
            v_ref[h, MLA_V:, :] = ones_row.astype(BF16)
    else:
        v_ref[...] = vals.astype(BF16)
    kslab = jnp.where(rope_lanes, zkr * cos + swap_halves(zkr) * sin, 0.0)
    kr_ref[n_prev] = kslab[:, MLA_NOPE:MLA_NOPE + MLA_ROPE]
    for h in range(MLA_HEADS):
        sl = slice(h * LANES, (h + 1) * LANES)
        k_ref[:, sl] = (kn[:, sl] + kslab).astype(BF16)

    kscale = RET_DK ** -0.5
    for s in range(2):
        sl = slice(s * LANES, (s + 1) * LANES)
        zq_s, zk_s = zr[:, sl], zr[:, 256 + s * LANES:256 + (s + 1) * LANES]
        rq_ref[:, sl] = (zq_s * cos + swap_halves(zq_s) * sin).astype(BF16)
        rk_ref[:, sl] = ((zk_s * cos + swap_halves(zk_s) * sin) * kscale).astype(BF16)
    rv_ref[...] = zv.astype(BF16)
    rg_ref[...] = (g * jax.nn.sigmoid(g)).astype(BF16)


def _proj(h, tab, norm_w, q_norm_w, kv_norm_w, w, layer, prev, *, tm, n_pos, n_rep, v_rows):
    N = h.shape[0]
    row = lambda s, b: (b * n_pos + s, 0)
    stacked = lambda s, b: (0, b * n_pos + s, 0)
    outs = [(MLA_HEADS * LANES, BF16), (MLA_HEADS * LANES, BF16), (MLA_HEADS * MLA_V, BF16),
            (MLA_KV_LORA, F32), (MLA_ROPE, F32), (256, BF16), (256, BF16), (512, BF16), (512, BF16)]
    out_specs = [pl.BlockSpec((tm, c), row) for c, _ in outs]
    out_shape = [jax.ShapeDtypeStruct((N, c), dt) for c, dt in outs]
    for n in (3, 4):
        out_specs[n] = pl.BlockSpec((layer + 1, tm, outs[n][0]), stacked)
        out_shape[n] = jax.ShapeDtypeStruct((layer + 1, N, outs[n][0]), F32)
    wv = w['wv']
    if v_rows:
        out_specs[2] = pl.BlockSpec((None, MLA_HEADS, V_ROWS, tm), lambda s, b: (b * n_pos + s, 0, 0, 0))
        out_shape[2] = jax.ShapeDtypeStruct((N // tm, MLA_HEADS, V_ROWS, tm), BF16)
        wv = w['wvt']
    prev = () if prev is None else prev
    assert len(prev) == (2 if layer else 0)
    return pl.pallas_call(
        functools.partial(_proj_kernel, v_rows=v_rows, n_prev=layer),
        grid=(n_pos, n_rep),
        in_specs=[
            pl.BlockSpec((tm, D_MODEL), row),
            pl.BlockSpec((2, tm, LANES), lambda s, b: (0, s, 0)),
            _layer_spec(norm_w, layer), _layer_spec(w['win'], layer),
            _layer_spec(q_norm_w, layer), _layer_spec(w['wq'], layer),
            _layer_spec(kv_norm_w, layer), _layer_spec(w['wk'], layer), _layer_spec(wv, layer),
        ] + [pl.BlockSpec((layer, tm, x.shape[-1]), stacked) for x in prev],
        out_specs=out_specs,
        out_shape=out_shape,
        compiler_params=_params("arbitrary", "arbitrary"),
        name="proj",
    )(h, tab, norm_w, w['win'], q_norm_w, w['wq'], kv_norm_w, w['wk'], wv, *prev)


def _visible_cap(t):
    kc = np.arange(t)[:, None] // CHUNK
    qc = np.arange(t)[None, :] // CHUNK
    return jnp.asarray(np.where(kc <= qc, np.finfo(np.float32).max, NEG), F32)


def _attn_prompt_kernel(q_ref, k_ref, vt_ref, cap_ref, o_ref, s_ref, acc_ref, m_ref, *, t, n_tiles):
    acc_ref[...] = jnp.zeros(acc_ref.shape, F32)
    m_ref[...] = jnp.full(m_ref.shape, NEG, F32)

    def produce(item, slot, masked):
        qi, kj = item
        q0 = pl.multiple_of(qi * t, t)
        k0 = pl.multiple_of(kj * t, t)
        out = []
        for hh in range(2):
            sl = slice(hh * LANES, (hh + 1) * LANES)
            s = _dot_nt(k_ref[pl.ds(k0, t), sl], q_ref[pl.ds(q0, t), sl])
            if masked:
                cols = []
                for lt in range(t // LANES):
                    r0 = (lt * LANES // CHUNK + 1) * CHUNK
                    col = s[:, lt * LANES:(lt + 1) * LANES]
                    capped = jnp.minimum(col[r0:], cap_ref[r0:, lt * LANES:(lt + 1) * LANES])
                    cols.append(jnp.concatenate([col[:r0], capped], axis=0))
                s = jnp.concatenate(cols, axis=1)
            s_ref[slot, hh] = s
            out.append(jnp.max(s, axis=0, keepdims=True))
        return tuple(out)

    def consume(smax, slot, item):
        qi, kj = item
        for hh in range(2):
            m = m_ref[qi, hh]
            m_new = jnp.maximum(m, smax[hh])
            p = jnp.exp2(s_ref[slot, hh] - m_new).astype(BF16)
            acc_ref[qi, hh] = jnp.exp2(m - m_new) * acc_ref[qi, hh] + _dot(vt_ref[kj, hh], p)
            m_ref[qi, hh] = m_new

    def run(first, count, advance, masked):
        if count == 0:
            return

        def step(carry, slot):
            smax, prev, cur = carry
            nxt = produce(cur, 1 - slot, masked)
            consume(smax, slot, prev)
            return nxt, cur, advance(cur)

        def steps(r, carry):
            for u in range(STEPS_PER_ITER):
                carry = step(carry, u % 2)
            return carry

        carry = lax.fori_loop(0, (count - 1) // STEPS_PER_ITER, steps,
                              (produce(first, 0, masked), first, advance(first)))
        left = (count - 1) % STEPS_PER_ITER
        for u in range(left):
            carry = step(carry, u % 2)
        consume(carry[0], left % 2, carry[1])

    def next_below(item):
        qi, kj = item
        last = kj == qi - 1
        return jnp.where(last, qi + 1, qi), jnp.where(last, 0, kj + 1)

    run((jnp.int32(1), jnp.int32(0)), n_tiles * (n_tiles - 1) // 2, next_below, False)
    run((jnp.int32(0), jnp.int32(0)), n_tiles, lambda item: (item[0] + 1, item[1] + 1), True)

    def finish(qi, carry):
        heads = [acc_ref[qi, hh, :MLA_V] * (1.0 / acc_ref[qi, hh, MLA_V:MLA_V + 1]) for hh in range(2)]
        o_ref[pl.ds(pl.multiple_of(qi * t, t), t), :] = jnp.concatenate(heads, axis=0).T.astype(BF16)
        return carry

    lax.fori_loop(0, n_tiles, finish, 0)


def _attn_prompt(q, k, vt, *, t):
    B, S, _ = q.shape
    n_tiles = S // t
    return pl.pallas_call(
        functools.partial(_attn_prompt_kernel, t=t, n_tiles=n_tiles),
        grid=(B, HEAD_PAIRS),
        in_specs=[pl.BlockSpec((None, S, 2 * LANES), lambda b, p: (b, 0, p)),
                  pl.BlockSpec((None, S, 2 * LANES), lambda b, p: (b, 0, p)),
                  pl.BlockSpec((None, n_tiles, 2, V_ROWS, t), lambda b, p: (b, 0, p, 0, 0)),
                  _const_spec((t, t))],
        out_specs=pl.BlockSpec((None, S, LANES), lambda b, p: (b, 0, p)),
        out_shape=jax.ShapeDtypeStruct((B, S, MLA_HEADS * MLA_V), BF16),
        scratch_shapes=[pltpu.VMEM((2, 2, t, t), F32), pltpu.VMEM((n_tiles, 2, V_ROWS, t), F32),
                        pltpu.VMEM((n_tiles, 2, 1, t), F32)],
        compiler_params=_params("arbitrary", "arbitrary"),
        name="attn_prompt",
    )(q, k, vt, _visible_cap(t))


def _attn_sample_kernel(q_ref, cn_ref, rn_ref, cc_ref, rc_ref, wk_ref, wv_ref, o_ref):
    L = q_ref.shape[0]
    q_lat, q_rot = [], []
    for h in range(MLA_HEADS):
        qh = q_ref[:, h * LANES:(h + 1) * LANES]
        q_lat.append(_dot_nt(qh, wk_ref[:, h * LANES:(h + 1) * LANES]))
        q_rot.append(qh[:, MLA_NOPE:MLA_NOPE + MLA_ROPE])
    q_lat = jnp.concatenate(q_lat, axis=0).astype(BF16)
    q_rot = jnp.concatenate(q_rot, axis=0)
    cc, rc = cc_ref[...].astype(BF16), rc_ref[...].astype(BF16)
    cn, rn = cn_ref[...].astype(BF16), rn_ref[...].astype(BF16)
    sc = _dot_nt(q_lat, cc) + _dot_nt(q_rot, rc)
    sn = _dot_nt(q_lat, cn) + _dot_nt(q_rot, rn)
    m = jnp.maximum(jnp.max(sc, axis=-1, keepdims=True), jnp.max(sn, axis=-1, keepdims=True))
    pc, pn = jnp.exp2(sc - m), jnp.exp2(sn - m)
    l = jnp.sum(pc, axis=-1, keepdims=True) + jnp.sum(pn, axis=-1, keepdims=True)
    o_lat = ((_dot(pc.astype(BF16), cc) + _dot(pn.astype(BF16), cn)) / l).astype(BF16)
    lane = lax.broadcasted_iota(jnp.int32, (L, LANES), 1)
    for p in range(HEAD_PAIRS):
        wvp = wv_ref[:, p * LANES:(p + 1) * LANES]
        even = _dot(o_lat[2 * p * L:(2 * p + 1) * L], wvp)
        odd = _dot(o_lat[(2 * p + 1) * L:(2 * p + 2) * L], wvp)
        o_ref[:, p * LANES:(p + 1) * LANES] = jnp.where(lane < MLA_V, even, odd).astype(BF16)


def _attn_sample(q, lat_new, cache_c, cache_r, w, layer, *, past):
    B, L, _ = q.shape
    new = lambda b: (layer, b, 0)
    old = lambda b: (layer * B + b, 0)
    return pl.pallas_call(
        _attn_sample_kernel,
        grid=(B,),
        in_specs=[pl.BlockSpec((None, L, MLA_HEADS * LANES), lambda b: (b, 0, 0)),
                  pl.BlockSpec((None, L, MLA_KV_LORA), new), pl.BlockSpec((None, L, MLA_ROPE), new),
                  pl.BlockSpec((past, MLA_KV_LORA), old), pl.BlockSpec((past, MLA_ROPE), old),
                  _layer_spec(w['wk'], layer), _layer_spec(w['wv'], layer)],
        out_specs=pl.BlockSpec((None, L, MLA_HEADS * MLA_V), lambda b: (b, 0, 0)),
        out_shape=jax.ShapeDtypeStruct((B, L, MLA_HEADS * MLA_V), BF16),
        compiler_params=_params("arbitrary"),
        name="attn_sample",
    )(q, *lat_new, cache_c, cache_r, w['wk'], w['wv'])


def _ret_kernel(rq_ref, rk_ref, rv_ref, t0_ref, dec_ref, qd_ref, kd_ref, rs_ref, o_ref, tn_ref, st_ref):
    L = rq_ref.shape[0]
    blk = pl.program_id(1)

    def head_block(h):
        r0, c0 = (h % 4) * RET_DK, (h % 2) * RET_DV
        return h // 2, slice(r0, r0 + RET_DK), slice(c0, c0 + RET_DV)

    @pl.when(blk == 0)
    def _():
        st_ref[...] = jnp.zeros(st_ref.shape, F32)
        for h in range(RET_HEADS):
            st_ref[head_block(h)] = t0_ref[h]

    lane = lax.broadcasted_iota(jnp.int32, (L, LANES), 1)
    low = lane < RET_DV
    srow = lax.broadcasted_iota(jnp.int32, (LANES, LANES), 0) // RET_DK
    scol = lax.broadcasted_iota(jnp.int32, (LANES, LANES), 1) // RET_DV
    qk = [(rq_ref[:, s * LANES:(s + 1) * LANES], rk_ref[:, s * LANES:(s + 1) * LANES]) for s in range(2)]
    vps = [rv_ref[:, p * LANES:(p + 1) * LANES] for p in range(HEAD_PAIRS)]
    states = [st_ref[p] for p in range(HEAD_PAIRS)]

    cross = [_dot((qk[p // 2][0].astype(F32) * qd_ref[p // 2]).astype(BF16), states[p].astype(BF16))
             for p in range(HEAD_PAIRS)]
    raw = []
    for h in range(RET_HEADS):
        qs, ks = qk[h // 4]
        raw.append(_dot_nt(jnp.where(lane // RET_DK == h % 4, qs, jnp.zeros_like(qs)), ks))
    for p in range(HEAD_PAIRS):
        ks = qk[p // 2][1]
        kdec = (ks.astype(F32) * kd_ref[p // 2]).astype(BF16)
        new_state = rs_ref[p // 2] * states[p] + _dot_tn(kdec, vps[p])
        st_ref[p] = jnp.where(srow == 2 * (p % 2) + scol, new_state, 0.0)

    for p in range(HEAD_PAIRS):
        o = cross[p]
        for c in range(2):
            h = 2 * p + c
            oh = _dot((raw[h] * dec_ref[h]).astype(BF16), vps[p])
            o = o + jnp.where(low if c == 0 else ~low, oh, 0.0)
        o_ref[:, p * LANES:(p + 1) * LANES] = o.astype(BF16)

    @pl.when(blk == pl.num_programs(1) - 1)
    def _():
        for h in range(RET_HEADS):
            tn_ref[h] = st_ref[head_block(h)]


def _retention(rq, rk, rv, t0, *, L):
    B, S, _ = rq.shape
    dec, qd, kd, rs = _retention_tables(L)
    blk = lambda b, i: (b, i, 0)
    st = lambda b, i: (b, 0, 0, 0)
    state_spec = pl.BlockSpec((None, RET_HEADS, RET_DK, RET_DV), st)
    return pl.pallas_call(
        _ret_kernel,
        grid=(B, S // L),
        in_specs=[pl.BlockSpec((None, L, 256), blk), pl.BlockSpec((None, L, 256), blk),
                  pl.BlockSpec((None, L, 512), blk), state_spec,
                  _const_spec(dec.shape), _const_spec(qd.shape), _const_spec(kd.shape), _const_spec(rs.shape)],
        out_specs=[pl.BlockSpec((None, L, 512), blk), state_spec],
        out_shape=[jax.ShapeDtypeStruct((B, S, 512), BF16),
                   jax.ShapeDtypeStruct((B, RET_HEADS, RET_DK, RET_DV), t0.dtype)],
        scratch_shapes=[pltpu.VMEM((HEAD_PAIRS, LANES, LANES), F32)],
        compiler_params=_params("arbitrary", "arbitrary"),
        name="retention",
    )(rq, rk, rv, t0, dec, qd, kd, rs)


def _post_kernel(h_ref, att_ref, ret_ref, g_ref, p_ref, gnw_ref, woa_ref, wor_ref, nf_ref, w1_ref, w2_ref,
                 np_ref, wg_ref, wp_ref, fin_ref, o_ref, *, final, fc):
    ret = []
    for p in range(HEAD_PAIRS):
        sl = slice(p * LANES, (p + 1) * LANES)
        on = _group_norm_pair(ret_ref[:, sl].astype(F32), gnw_ref[:, sl])
        ret.append((g_ref[:, sl].astype(F32) * on).astype(BF16))
    ret = jnp.concatenate(ret, axis=1)
    h = h_ref[...] + _dot(att_ref[...], woa_ref[...]) + _dot(ret, wor_ref[...])
    a = _rms(h, nf_ref[...]).astype(BF16)
    acc = jnp.zeros_like(h)
    for c in range(D_FF // fc):
        u = _dot(a, w1_ref[:, c * fc:(c + 1) * fc])
        acc = acc + _dot(jnp.square(jnp.maximum(u, 0.0)).astype(BF16), w2_ref[c * fc:(c + 1) * fc, :])
    h = h + acc
    gate = jax.nn.sigmoid(_dot(_rms(h, np_ref[...]).astype(BF16), wg_ref[...]))
    h = h + _dot(p_ref[...].astype(BF16), wp_ref[...]) * gate
    o_ref[...] = _rms(h, fin_ref[...]) if final else h


def _post(h, att, ret, g, p, gn_w, norm_ffn_w, norm_ple_w, final_norm_w, w, layer, *, tm, final):
    N = h.shape[0]
    row = lambda i: (i, 0)
    return pl.pallas_call(
        functools.partial(_post_kernel, final=final, fc=512),
        grid=(N // tm,),
        in_specs=[pl.BlockSpec((tm, D_MODEL), row), pl.BlockSpec((tm, 512), row),
                  pl.BlockSpec((tm, 512), row), pl.BlockSpec((tm, 512), row),
                  pl.BlockSpec((tm, PLE_DIM), lambda i: (layer * (N // tm) + i, 0)),
                  _layer_spec(gn_w, layer), _layer_spec(w['woa'], layer), _layer_spec(w['wor'], layer), _layer_spec(norm_ffn_w, layer),
                  _layer_spec(w['w1'], layer), _layer_spec(w['w2'], layer), _layer_spec(norm_ple_w, layer),
                  _layer_spec(w['wg'], layer), _layer_spec(w['wp'], layer), _const_spec((1, D_MODEL))],
        out_specs=pl.BlockSpec((tm, D_MODEL), row),
        out_shape=jax.ShapeDtypeStruct((N, D_MODEL), F32),
        compiler_params=_params("arbitrary"),
        name="post",
    )(h, att, ret, g, p, gn_w, w['woa'], w['wor'], norm_ffn_w, w['w1'], w['w2'], norm_ple_w, w['wg'],
      w['wp'], final_norm_w.reshape(1, -1))


PROMPT_TM = 512
RET_BLOCK = 256
STEPS_PER_ITER = 16


def kernel(x_prompt, x_sample, cache_ckv, cache_krope, state_ret, p_prompt, p_sample, norm_mix_w, w_in,
           q_norm_w, w_uq, kv_norm_w, w_ukv, ret_gn_w, w_out, norm_ffn_w, w_ff1, w_ff2, norm_ple_w,
           w_ple_gate, w_ple_proj, final_norm_w):
    B, S, D = x_prompt.shape
    Bd, Ld, _ = x_sample.shape
    depth, _, past, _ = cache_ckv.shape
    tm_p = min(PROMPT_TM, S)
    assert past % CHUNK == 0 and Ld <= CHUNK and S % tm_p == 0 and tm_p % CHUNK == 0
    Np, Ns = B * S, Bd * Ld

    w = _prep_weights(w_in, w_uq, w_ukv, w_out, w_ff1, w_ff2, w_ple_gate, w_ple_proj)
    norm_mix_w, q_norm_w, kv_norm_w, ret_gn_w, norm_ffn_w, norm_ple_w = map(
        _rows, (norm_mix_w, q_norm_w, kv_norm_w, ret_gn_w, norm_ffn_w, norm_ple_w))
    tab_p = _rope_table(jnp.arange(S, dtype=jnp.int32))
    tab_s = jnp.tile(_rope_table(past + jnp.arange(Ld, dtype=jnp.int32)), (1, Bd, 1))
    zero_state = jnp.zeros((B, RET_HEADS, RET_DK, RET_DV), x_prompt.dtype)
    pp = p_prompt.reshape(depth * Np, PLE_DIM)
    ps = p_sample.reshape(depth * Ns, PLE_DIM)
    cache_c = cache_ckv.reshape(depth * Bd * past, MLA_KV_LORA)
    cache_r = cache_krope.reshape(depth * Bd * past, MLA_ROPE)

    hp = x_prompt.reshape(Np, D)
    hs = x_sample.reshape(Ns, D)
    lat_p = lat_s = None
    st_p, st_s = [], []
    for i in range(depth):
        final = i == depth - 1

        q, k, v, ckv, kr, rq, rk, rv, rg = _proj(hp, tab_p, norm_mix_w, q_norm_w, kv_norm_w, w, i, lat_p,
                                                 tm=tm_p, n_pos=S // tm_p, n_rep=B, v_rows=True)
        lat_p = (ckv, kr)
        att = _attn_prompt(q.reshape(B, S, -1), k.reshape(B, S, -1),
                           v.reshape(B, S // tm_p, MLA_HEADS, V_ROWS, tm_p), t=tm_p)
        ret, st = _retention(rq.reshape(B, S, -1), rk.reshape(B, S, -1), rv.reshape(B, S, -1), zero_state,
                             L=min(RET_BLOCK, S))
        hp = _post(hp, att.reshape(Np, -1), ret.reshape(Np, -1), rg, pp, ret_gn_w, norm_ffn_w, norm_ple_w,
                   final_norm_w, w, i, tm=tm_p, final=final)
        st_p.append(st)

        q, k, v, ckv, kr, rq, rk, rv, rg = _proj(hs, tab_s, norm_mix_w, q_norm_w, kv_norm_w, w, i, lat_s,
                                                 tm=Ns, n_pos=1, n_rep=1, v_rows=False)
        lat_s = (ckv, kr)
        att = _attn_sample(q.reshape(Bd, Ld, -1), lat_s, cache_c, cache_r, w, i, past=past)
        ret, st = _retention(rq.reshape(Bd, Ld, -1), rk.reshape(Bd, Ld, -1), rv.reshape(Bd, Ld, -1),
                             state_ret[i], L=Ld)
        hs = _post(hs, att.reshape(Ns, -1), ret.reshape(Ns, -1), rg, ps, ret_gn_w, norm_ffn_w, norm_ple_w,
                   final_norm_w, w, i, tm=Ns, final=final)
        st_s.append(st)

    return (hp.reshape(B, S, D), hs.reshape(Bd, Ld, D),
            lat_p[0].reshape(depth, B, S, -1), lat_p[1].reshape(depth, B, S, -1), jnp.stack(st_p),
            lat_s[0].reshape(depth, Bd, Ld, -1), lat_s[1].reshape(depth, Bd, Ld, -1), jnp.stack(st_s))
```

```python
import functools

import numpy as np
import jax
import jax.numpy as jnp
from jax import lax
from jax.experimental import pallas as pl
from jax.experimental.pallas import tpu as pltpu

D_MODEL = 1024
CHUNK = 64
PLE_DIM = 256
D_FF = 4 * D_MODEL
MLA_HEADS = 8
MLA_NOPE = 64
MLA_ROPE = 32
MLA_V = 64
MLA_Q_LORA = 384
MLA_KV_LORA = 256
MLA_SCALE = (MLA_NOPE + MLA_ROPE) ** -0.5
Q_SCALE = MLA_SCALE * float(np.log2(np.e))
RET_HEADS = 8
RET_DK = 32
RET_DV = 64
ROPE_THETA = 10000.0
EPS = 1e-6
NEG = -1e30
IN_SPLITS = (MLA_Q_LORA, MLA_KV_LORA, MLA_ROPE, RET_HEADS * RET_DK, RET_HEADS * RET_DK,
             RET_HEADS * RET_DV, RET_HEADS * RET_DV)

LANES = 128
HEAD_PAIRS = MLA_HEADS // 2
V_ROWS = MLA_V + 16
VMEM_LIMIT = 56 * 1024 * 1024

_C_Q, _C_C, _C_R, _C_RV, _C_RG, _C_KR, _C_END = 0, 384, 640, 1152, 1664, 2176, 2304

F32 = jnp.float32
BF16 = jnp.bfloat16


def _pad_cols(w, left, width):
    lead, n = w.shape[:-1], w.shape[-1]
    return jnp.concatenate([jnp.zeros((*lead, left), w.dtype), w,
                            jnp.zeros((*lead, width - left - n), w.dtype)], axis=-1)


def _prep_weights(w_in, w_uq, w_ukv, w_out, w_ff1, w_ff2, w_ple_gate, w_ple_proj):
    depth = w_in.shape[0]
    offs = np.cumsum((0,) + IN_SPLITS)
    q_lat, ckv, kr, rq, rk, rv, rg = [w_in[..., offs[n]:offs[n + 1]] for n in range(len(IN_SPLITS))]
    win = jnp.concatenate([q_lat, ckv, rq, rk, rv, rg, _pad_cols(kr, MLA_NOPE, LANES)], axis=-1)
    assert win.shape[-1] == _C_END
    uq = w_uq.reshape(depth, MLA_Q_LORA, MLA_HEADS, MLA_NOPE + MLA_ROPE)
    wq = _pad_cols(uq, 0, LANES).reshape(depth, MLA_Q_LORA, MLA_HEADS * LANES)
    ukv = w_ukv.reshape(depth, MLA_KV_LORA, MLA_HEADS, MLA_NOPE + MLA_V)
    wv = ukv[..., MLA_NOPE:].reshape(depth, MLA_KV_LORA, MLA_HEADS * MLA_V)
    return dict(
        win=win.astype(BF16),
        wq=wq.astype(BF16),
        wk=_pad_cols(ukv[..., :MLA_NOPE], 0, LANES).reshape(depth, MLA_KV_LORA, MLA_HEADS * LANES).astype(BF16),
        wv=wv.astype(BF16),
        wvt=jnp.swapaxes(wv, 1, 2).astype(BF16),
        woa=w_out[:, :MLA_HEADS * MLA_V].astype(BF16),
        wor=w_out[:, MLA_HEADS * MLA_V:].astype(BF16),
        w1=w_ff1.astype(BF16),
        w2=w_ff2.astype(BF16),
        wg=w_ple_gate.astype(BF16),
        wp=w_ple_proj.astype(BF16),
    )


def _rope_table(pos):
    half = MLA_ROPE // 2
    inv = ROPE_THETA ** (-jnp.arange(half, dtype=F32) / half)
    ang = pos.astype(F32)[:, None] * inv[None, :]
    cos, sin = jnp.cos(ang), jnp.sin(ang)
    return jnp.stack([jnp.tile(cos, (1, LANES // half)), jnp.tile(sin, (1, LANES // half))])


def _retention_tables(L):
    log_gamma = np.log1p(-np.exp2(-5.0 - np.arange(RET_HEADS)))
    idx = np.arange(L, dtype=np.float64)
    diff = idx[:, None] - idx[None, :]
    dec = np.where(diff >= 0, np.exp(np.maximum(diff, 0.0)[None] * log_gamma[:, None, None]), 0.0)
    lg_lane = np.repeat(log_gamma, RET_DK).reshape(2, 1, LANES)
    qd = np.exp((idx + 1.0)[None, :, None] * lg_lane)
    kd = np.exp((L - 1.0 - idx)[None, :, None] * lg_lane)
    rs = np.broadcast_to(np.exp(L * lg_lane).reshape(2, LANES, 1), (2, LANES, LANES))
    return [jnp.asarray(x, F32) for x in (dec, qd, kd, rs)]


def _rms(x, w):
    return x * lax.rsqrt(jnp.mean(x * x, axis=-1, keepdims=True) + EPS) * w


def _group_norm_pair(o, gain):
    low = lax.broadcasted_iota(jnp.int32, o.shape, 1) < RET_DV

    def half_mean(x):
        lo = jnp.sum(jnp.where(low, x, 0.0), axis=-1, keepdims=True)
        hi = jnp.sum(jnp.where(low, 0.0, x), axis=-1, keepdims=True)
        return jnp.where(low, lo, hi) * (1.0 / RET_DV)

    d = o - half_mean(o)
    return d * lax.rsqrt(half_mean(d * d) + EPS) * gain


def _dot(a, b):
    return jnp.dot(a, b, preferred_element_type=F32)


def _dot_nt(a, b):
    return lax.dot_general(a, b, (((1,), (1,)), ((), ())), preferred_element_type=F32)


def _dot_tn(a, b):
    return lax.dot_general(a, b, (((0,), (0,)), ((), ())), preferred_element_type=F32)


def _const_spec(shape):
    zeros = (0,) * len(shape)
    return pl.BlockSpec(shape, lambda *_: zeros, pipeline_mode=pl.Buffered(1))


def _layer_spec(stacked, layer):
    zeros = (0,) * (stacked.ndim - 1)
    return pl.BlockSpec((None,) + stacked.shape[1:], lambda *_: (layer,) + zeros,
                        pipeline_mode=pl.Buffered(1))


def _rows(w):
    return w.reshape(w.shape[0], 1, -1)


def _params(*sem):
    return pltpu.CompilerParams(dimension_semantics=sem, vmem_limit_bytes=VMEM_LIMIT)


def _proj_kernel(h_ref, tab_ref, nw_ref, win_ref, qnw_ref, wq_ref, kvnw_ref, wk_ref, wv_ref,
                 *refs, v_rows, n_prev):
    if n_prev:
        ckv_prev_ref, kr_prev_ref, *refs = refs
    q_ref, k_ref, v_ref, ckv_ref, kr_ref, rq_ref, rk_ref, rv_ref, rg_ref = refs
    tm = h_ref.shape[0]
    cos, sin = tab_ref[0], tab_ref[1]
    lane = lax.broadcasted_iota(jnp.int32, (tm, LANES), 1)
    rope_lanes = (lane >= MLA_NOPE) & (lane < MLA_NOPE + MLA_ROPE)
    half = MLA_ROPE // 2
    first_half = lane % MLA_ROPE < half
    sin = jnp.where(first_half, -sin, sin)

    def swap_halves(x):
        return jnp.where(first_half, pltpu.roll(x, LANES - half, 1), pltpu.roll(x, half, 1))

    a = _rms(h_ref[...], nw_ref[...]).astype(BF16)

    zq = _dot(a, win_ref[:, _C_Q:_C_C])
    zc = _dot(a, win_ref[:, _C_C:_C_R])
    zr = _dot(a, win_ref[:, _C_R:_C_RV])
    zv = _dot(a, win_ref[:, _C_RV:_C_RG])
    g = _dot(a, win_ref[:, _C_RG:_C_KR])
    zkr = _dot(a, win_ref[:, _C_KR:_C_END])
    ql = _rms(zq, qnw_ref[...]).astype(BF16)
    c = _rms(zc, kvnw_ref[...])
    cb = c.astype(BF16)
    qz = _dot(ql, wq_ref[...])
    kn = _dot(cb, wk_ref[...])
    vals = _dot_nt(wv_ref[...], cb) if v_rows else _dot(cb, wv_ref[...])

    cq = jnp.where(lane < MLA_NOPE, Q_SCALE, jnp.where(rope_lanes, cos * Q_SCALE, 0.0))
    sq = jnp.where(rope_lanes, sin * Q_SCALE, 0.0)
    for h in range(MLA_HEADS):
        sl = slice(h * LANES, (h + 1) * LANES)
        q_ref[:, sl] = (qz[:, sl] * cq + swap_halves(qz[:, sl]) * sq).astype(BF16)

    if n_prev:
        ckv_ref[:n_prev] = ckv_prev_ref[...]
        kr_ref[:n_prev] = kr_prev_ref[...]
    ckv_ref[n_prev] = c
    if v_rows:
        ones_row = jnp.where(lax.broadcasted_iota(jnp.int32, (V_ROWS - MLA_V, tm), 0) == 0, 1.0, 0.0)
        for h in range(MLA_HEADS):
            v_ref[h, :MLA_V, :] = vals[h * MLA_V:(h + 1) * MLA_V, :].astype(BF16)
            v_ref[h, MLA_V:, :] = ones_row.astype(BF16)
    else:
        v_ref[...] = vals.astype(BF16)
    kslab = jnp.where(rope_lanes, zkr * cos + swap_halves(zkr) * sin, 0.0)
    kr_ref[n_prev] = kslab[:, MLA_NOPE:MLA_NOPE + MLA_ROPE]
    for h in range(MLA_HEADS):
        sl = slice(h * LANES, (h + 1) * LANES)
        k_ref[:, sl] = (kn[:, sl] + kslab).astype(BF16)

    kscale = RET_DK ** -0.5
    for s in range(2):
        sl = slice(s * LANES, (s + 1) * LANES)
        zq_s, zk_s = zr[:, sl], zr[:, 256 + s * LANES:256 + (s + 1) * LANES]
        rq_ref[:, sl] = (zq_s * cos + swap_halves(zq_s) * sin).astype(BF16)
        rk_ref[:, sl] = ((zk_s * cos + swap_halves(zk_s) * sin) * kscale).astype(BF16)
    rv_ref[...] = zv.astype(BF16)
    rg_ref[...] = (g * jax.nn.sigmoid(g)).astype(BF16)


def _proj(h, tab, norm_w, q_norm_w, kv_norm_w, w, layer, prev, *, tm, n_pos, n_rep, v_rows):
    N = h.shape[0]
    row = lambda s, b: (b * n_pos + s, 0)
    stacked = lambda s, b: (0, b * n_pos + s, 0)
    outs = [(MLA_HEADS * LANES, BF16), (MLA_HEADS * LANES, BF16), (MLA_HEADS * MLA_V, BF16),
            (MLA_KV_LORA, F32), (MLA_ROPE, F32), (256, BF16), (256, BF16), (512, BF16), (512, BF16)]
    out_specs = [pl.BlockSpec((tm, c), row) for c, _ in outs]
    out_shape = [jax.ShapeDtypeStruct((N, c), dt) for c, dt in outs]
    for n in (3, 4):
        out_specs[n] = pl.BlockSpec((layer + 1, tm, outs[n][0]), stacked)
        out_shape[n] = jax.ShapeDtypeStruct((layer + 1, N, outs[n][0]), F32)
    wv = w['wv']
    if v_rows:
        out_specs[2] = pl.BlockSpec((None, MLA_HEADS, V_ROWS, tm), lambda s, b: (b * n_pos + s, 0, 0, 0))
        out_shape[2] = jax.ShapeDtypeStruct((N // tm, MLA_HEADS, V_ROWS, tm), BF16)
        wv = w['wvt']
    prev = () if prev is None else prev
    assert len(prev) == (2 if layer else 0)
    return pl.pallas_call(
        functools.partial(_proj_kernel, v_rows=v_rows, n_prev=layer),
        grid=(n_pos, n_rep),
        in_specs=[
            pl.BlockSpec((tm, D_MODEL), row),
            pl.BlockSpec((2, tm, LANES), lambda s, b: (0, s, 0)),
            _layer_spec(norm_w, layer), _layer_spec(w['win'], layer),
            _layer_spec(q_norm_w, layer), _layer_spec(w['wq'], layer),
            _layer_spec(kv_norm_w, layer), _layer_spec(w['wk'], layer), _layer_spec(wv, layer),
        ] + [pl.BlockSpec((layer, tm, x.shape[-1]), stacked) for x in prev],
        out_specs=out_specs,
        out_shape=out_shape,
        compiler_params=_params("arbitrary", "arbitrary"),
        name="proj",
    )(h, tab, norm_w, w['win'], q_norm_w, w['wq'], kv_norm_w, w['wk'], wv, *prev)


def _visible_cap(t):
    kc = np.arange(t)[:, None] // CHUNK
    qc = np.arange(t)[None, :] // CHUNK
    return jnp.asarray(np.where(kc <= qc, np.finfo(np.float32).max, NEG), F32)


def _attn_prompt_kernel(q_ref, k_ref, vt_ref, cap_ref, o_ref, s_ref, acc_ref, m_ref, *, t, n_tiles):
    acc_ref[...] = jnp.zeros(acc_ref.shape, F32)
    m_ref[...] = jnp.full(m_ref.shape, NEG, F32)

    def produce(item, slot, masked):
        qi, kj = item
        q0 = pl.multiple_of(qi * t, t)
        k0 = pl.multiple_of(kj * t, t)
        out = []
        for hh in range(2):
            sl = slice(hh * LANES, (hh + 1) * LANES)
            s = _dot_nt(k_ref[pl.ds(k0, t), sl], q_ref[pl.ds(q0, t), sl])
            if masked:
                cols = []
                for lt in range(t // LANES):
                    r0 = (lt * LANES // CHUNK + 1) * CHUNK
                    col = s[:, lt * LANES:(lt + 1) * LANES]
                    capped = jnp.minimum(col[r0:], cap_ref[r0:, lt * LANES:(lt + 1) * LANES])
                    cols.append(jnp.concatenate([col[:r0], capped], axis=0))
                s = jnp.concatenate(cols, axis=1)
            s_ref[slot, hh] = s
            out.append(jnp.max(s, axis=0, keepdims=True))
        return tuple(out)

    def consume(smax, slot, item):
        qi, kj = item
        for hh in range(2):
            m = m_ref[qi, hh]
            m_new = jnp.maximum(m, smax[hh])
            p = jnp.exp2(s_ref[slot, hh] - m_new).astype(BF16)
            acc_ref[qi, hh] = jnp.exp2(m - m_new) * acc_ref[qi, hh] + _dot(vt_ref[kj, hh], p)
            m_ref[qi, hh] = m_new

    def run(first, count, advance, masked):
        if count == 0:
            return

        def step(carry, slot):
            smax, prev, cur = carry
            nxt = produce(cur, 1 - slot, masked)
            consume(smax, slot, prev)
            return nxt, cur, advance(cur)

        def steps(r, carry):
            for u in range(STEPS_PER_ITER):
                carry = step(carry, u % 2)
            return carry

        carry = lax.fori_loop(0, (count - 1) // STEPS_PER_ITER, steps,
                              (produce(first, 0, masked), first, advance(first)))
        left = (count - 1) % STEPS_PER_ITER
        for u in range(left):
            carry = step(carry, u % 2)
        consume(carry[0], left % 2, carry[1])

    def next_below(item):
        qi, kj = item
        last = kj == qi - 1
        return jnp.where(last, qi + 1, qi), jnp.where(last, 0, kj + 1)

    run((jnp.int32(1), jnp.int32(0)), n_tiles * (n_tiles - 1) // 2, next_below, False)
    run((jnp.int32(0), jnp.int32(0)), n_tiles, lambda item: (item[0] + 1, item[1] + 1), True)

    def finish(qi, carry):
        heads = [acc_ref[qi, hh, :MLA_V] * (1.0 / acc_ref[qi, hh, MLA_V:MLA_V + 1]) for hh in range(2)]
        o_ref[pl.ds(pl.multiple_of(qi * t, t), t), :] = jnp.concatenate(heads, axis=0).T.astype(BF16)
        return carry

    lax.fori_loop(0, n_tiles, finish, 0)


def _attn_prompt(q, k, vt, *, t):
    B, S, _ = q.shape
    n_tiles = S // t
    return pl.pallas_call(
        functools.partial(_attn_prompt_kernel, t=t, n_tiles=n_tiles),
        grid=(B, HEAD_PAIRS),
        in_specs=[pl.BlockSpec((None, S, 2 * LANES), lambda b, p: (b, 0, p)),
                  pl.BlockSpec((None, S, 2 * LANES), lambda b, p: (b, 0, p)),
                  pl.BlockSpec((None, n_tiles, 2, V_ROWS, t), lambda b, p: (b, 0, p, 0, 0)),
                  _const_spec((t, t))],
        out_specs=pl.BlockSpec((None, S, LANES), lambda b, p: (b, 0, p)),
        out_shape=jax.ShapeDtypeStruct((B, S, MLA_HEADS * MLA_V), BF16),
        scratch_shapes=[pltpu.VMEM((2, 2, t, t), F32), pltpu.VMEM((n_tiles, 2, V_ROWS, t), F32),
                        pltpu.VMEM((n_tiles, 2, 1, t), F32)],
        compiler_params=_params("arbitrary", "arbitrary"),
        name="attn_prompt",
    )(q, k, vt, _visible_cap(t))


def _attn_sample_kernel(q_ref, cn_ref, rn_ref, cc_ref, rc_ref, wk_ref, wv_ref, o_ref):
    L = q_ref.shape[0]
    q_lat, q_rot = [], []
    for h in range(MLA_HEADS):
        qh = q_ref[:, h * LANES:(h + 1) * LANES]
        q_lat.append(_dot_nt(qh, wk_ref[:, h * LANES:(h + 1) * LANES]))
        q_rot.append(qh[:, MLA_NOPE:MLA_NOPE + MLA_ROPE])
    q_lat = jnp.concatenate(q_lat, axis=0).astype(BF16)
    q_rot = jnp.concatenate(q_rot, axis=0)
    cc, rc = cc_ref[...].astype(BF16), rc_ref[...].astype(BF16)
    cn, rn = cn_ref[...].astype(BF16), rn_ref[...].astype(BF16)
    sc = _dot_nt(q_lat, cc) + _dot_nt(q_rot, rc)
    sn = _dot_nt(q_lat, cn) + _dot_nt(q_rot, rn)
    m = jnp.maximum(jnp.max(sc, axis=-1, keepdims=True), jnp.max(sn, axis=-1, keepdims=True))
    pc, pn = jnp.exp2(sc - m), jnp.exp2(sn - m)
    l = jnp.sum(pc, axis=-1, keepdims=True) + jnp.sum(pn, axis=-1, keepdims=True)
    o_lat = ((_dot(pc.astype(BF16), cc) + _dot(pn.astype(BF16), cn)) / l).astype(BF16)
    lane = lax.broadcasted_iota(jnp.int32, (L, LANES), 1)
    for p in range(HEAD_PAIRS):
        wvp = wv_ref[:, p * LANES:(p + 1) * LANES]
        even = _dot(o_lat[2 * p * L:(2 * p + 1) * L], wvp)
        odd = _dot(o_lat[(2 * p + 1) * L:(2 * p + 2) * L], wvp)
        o_ref[:, p * LANES:(p + 1) * LANES] = jnp.where(lane < MLA_V, even, odd).astype(BF16)


def _attn_sample(q, lat_new, cache_c, cache_r, w, layer, *, past):
    B, L, _ = q.shape
    new = lambda b: (layer, b, 0)
    old = lambda b: (layer * B + b, 0)
    return pl.pallas_call(
        _attn_sample_kernel,
        grid=(B,),
        in_specs=[pl.BlockSpec((None, L, MLA_HEADS * LANES), lambda b: (b, 0, 0)),
                  pl.BlockSpec((None, L, MLA_KV_LORA), new), pl.BlockSpec((None, L, MLA_ROPE), new),
                  pl.BlockSpec((past, MLA_KV_LORA), old), pl.BlockSpec((past, MLA_ROPE), old),
                  _layer_spec(w['wk'], layer), _layer_spec(w['wv'], layer)],
        out_specs=pl.BlockSpec((None, L, MLA_HEADS * MLA_V), lambda b: (b, 0, 0)),
        out_shape=jax.ShapeDtypeStruct((B, L, MLA_HEADS * MLA_V), BF16),
        compiler_params=_params("arbitrary"),
        name="attn_sample",
    )(q, *lat_new, cache_c, cache_r, w['wk'], w['wv'])


def _ret_kernel(rq_ref, rk_ref, rv_ref, t0_ref, dec_ref, qd_ref, kd_ref, rs_ref, o_ref, tn_ref, st_ref):
    L = rq_ref.shape[0]
    blk = pl.program_id(1)

    def head_block(h):
        r0, c0 = (h % 4) * RET_DK, (h % 2) * RET_DV
        return h // 2, slice(r0, r0 + RET_DK), slice(c0, c0 + RET_DV)

    @pl.when(blk == 0)
    def _():
        st_ref[...] = jnp.zeros(st_ref.shape, F32)
        for h in range(RET_HEADS):
            st_ref[head_block(h)] = t0_ref[h]

    lane = lax.broadcasted_iota(jnp.int32, (L, LANES), 1)
    low = lane < RET_DV
    srow = lax.broadcasted_iota(jnp.int32, (LANES, LANES), 0) // RET_DK
    scol = lax.broadcasted_iota(jnp.int32, (LANES, LANES), 1) // RET_DV
    qk = [(rq_ref[:, s * LANES:(s + 1) * LANES], rk_ref[:, s * LANES:(s + 1) * LANES]) for s in range(2)]
    vps = [rv_ref[:, p * LANES:(p + 1) * LANES] for p in range(HEAD_PAIRS)]
    states = [st_ref[p] for p in range(HEAD_PAIRS)]

    cross = [_dot((qk[p // 2][0].astype(F32) * qd_ref[p // 2]).astype(BF16), states[p].astype(BF16))
             for p in range(HEAD_PAIRS)]
    raw = []
    for h in range(RET_HEADS):
        qs, ks = qk[h // 4]
        raw.append(_dot_nt(jnp.where(lane // RET_DK == h % 4, qs, jnp.zeros_like(qs)), ks))
    for p in range(HEAD_PAIRS):
        ks = qk[p // 2][1]
        kdec = (ks.astype(F32) * kd_ref[p // 2]).astype(BF16)
        new_state = rs_ref[p // 2] * states[p] + _dot_tn(kdec, vps[p])
        st_ref[p] = jnp.where(srow == 2 * (p % 2) + scol, new_state, 0.0)

    for p in range(HEAD_PAIRS):
        o = cross[p]
        for c in range(2):
            h = 2 * p + c
            oh = _dot((raw[h] * dec_ref[h]).astype(BF16), vps[p])
            o = o + jnp.where(low if c == 0 else ~low, oh, 0.0)
        o_ref[:, p * LANES:(p + 1) * LANES] = o.astype(BF16)

    @pl.when(blk == pl.num_programs(1) - 1)
    def _():
        for h in range(RET_HEADS):
            tn_ref[h] = st_ref[head_block(h)]


def _retention(rq, rk, rv, t0, *, L):
    B, S, _ = rq.shape
    dec, qd, kd, rs = _retention_tables(L)
    blk = lambda b, i: (b, i, 0)
    st = lambda b, i: (b, 0, 0, 0)
    state_spec = pl.BlockSpec((None, RET_HEADS, RET_DK, RET_DV), st)
    return pl.pallas_call(
        _ret_kernel,
        grid=(B, S // L),
        in_specs=[pl.BlockSpec((None, L, 256), blk), pl.BlockSpec((None, L, 256), blk),
                  pl.BlockSpec((None, L, 512), blk), state_spec,
                  _const_spec(dec.shape), _const_spec(qd.shape), _const_spec(kd.shape), _const_spec(rs.shape)],
        out_specs=[pl.BlockSpec((None, L, 512), blk), state_spec],
        out_shape=[jax.ShapeDtypeStruct((B, S, 512), BF16),
                   jax.ShapeDtypeStruct((B, RET_HEADS, RET_DK, RET_DV), t0.dtype)],
        scratch_shapes=[pltpu.VMEM((HEAD_PAIRS, LANES, LANES), F32)],
        compiler_params=_params("arbitrary", "arbitrary"),
        name="retention",
    )(rq, rk, rv, t0, dec, qd, kd, rs)


def _post_kernel(h_ref, att_ref, ret_ref, g_ref, p_ref, gnw_ref, woa_ref, wor_ref, nf_ref, w1_ref, w2_ref,
                 np_ref, wg_ref, wp_ref, fin_ref, o_ref, *, final, fc):
    ret = []
    for p in range(HEAD_PAIRS):
        sl = slice(p * LANES, (p + 1) * LANES)
        on = _group_norm_pair(ret_ref[:, sl].astype(F32), gnw_ref[:, sl])
        ret.append((g_ref[:, sl].astype(F32) * on).astype(BF16))
    ret = jnp.concatenate(ret, axis=1)
    h = h_ref[...] + _dot(att_ref[...], woa_ref[...])
    ple = _dot(p_ref[...].astype(BF16), wp_ref[...])
    h = h + _dot(ret, wor_ref[...])
    a = _rms(h, nf_ref[...]).astype(BF16)
    acc = jnp.zeros_like(h)
    for c in range(D_FF // fc):
        u = _dot(a, w1_ref[:, c * fc:(c + 1) * fc])
        acc = acc + _dot(jnp.square(jnp.maximum(u, 0.0)).astype(BF16), w2_ref[c * fc:(c + 1) * fc, :])
    h = h + acc
    gate = jax.nn.sigmoid(_dot(_rms(h, np_ref[...]).astype(BF16), wg_ref[...]))
    h = h + ple * gate
    o_ref[...] = _rms(h, fin_ref[...]) if final else h


def _post(h, att, ret, g, p, gn_w, norm_ffn_w, norm_ple_w, final_norm_w, w, layer, *, tm, final):
    N = h.shape[0]
    row = lambda i: (i, 0)
    return pl.pallas_call(
        functools.partial(_post_kernel, final=final, fc=512),
        grid=(N // tm,),
        in_specs=[pl.BlockSpec((tm, D_MODEL), row), pl.BlockSpec((tm, 512), row),
                  pl.BlockSpec((tm, 512), row), pl.BlockSpec((tm, 512), row),
                  pl.BlockSpec((tm, PLE_DIM), lambda i: (layer * (N // tm) + i, 0)),
                  _layer_spec(gn_w, layer), _layer_spec(w['woa'], layer), _layer_spec(w['wor'], layer), _layer_spec(norm_ffn_w, layer),
                  _layer_spec(w['w1'], layer), _layer_spec(w['w2'], layer), _layer_spec(norm_ple_w, layer),
                  _layer_spec(w['wg'], layer), _layer_spec(w['wp'], layer), _const_spec((1, D_MODEL))],
        out_specs=pl.BlockSpec((tm, D_MODEL), row),
        out_shape=jax.ShapeDtypeStruct((N, D_MODEL), F32),
        compiler_params=_params("arbitrary"),
        name="post",
    )(h, att, ret, g, p, gn_w, w['woa'], w['wor'], norm_ffn_w, w['w1'], w['w2'], norm_ple_w, w['wg'],
      w['wp'], final_norm_w.reshape(1, -1))


PROMPT_TM = 512
RET_BLOCK = 256
STEPS_PER_ITER = 16


def kernel(x_prompt, x_sample, cache_ckv, cache_krope, state_ret, p_prompt, p_sample, norm_mix_w, w_in,
           q_norm_w, w_uq, kv_norm_w, w_ukv, ret_gn_w, w_out, norm_ffn_w, w_ff1, w_ff2, norm_ple_w,
           w_ple_gate, w_ple_proj, final_norm_w):
    B, S, D = x_prompt.shape
    Bd, Ld, _ = x_sample.shape
    depth, _, past, _ = cache_ckv.shape
    tm_p = min(PROMPT_TM, S)
    assert past % CHUNK == 0 and Ld <= CHUNK and S % tm_p == 0 and tm_p % CHUNK == 0
    Np, Ns = B * S, Bd * Ld

    w = _prep_weights(w_in, w_uq, w_ukv, w_out, w_ff1, w_ff2, w_ple_gate, w_ple_proj)
    norm_mix_w, q_norm_w, kv_norm_w, ret_gn_w, norm_ffn_w, norm_ple_w = map(
        _rows, (norm_mix_w, q_norm_w, kv_norm_w, ret_gn_w, norm_ffn_w, norm_ple_w))
    tab_p = _rope_table(jnp.arange(S, dtype=jnp.int32))
    tab_s = jnp.tile(_rope_table(past + jnp.arange(Ld, dtype=jnp.int32)), (1, Bd, 1))
    zero_state = jnp.zeros((B, RET_HEADS, RET_DK, RET_DV), x_prompt.dtype)
    pp = p_prompt.reshape(depth * Np, PLE_DIM)
    ps = p_sample.reshape(depth * Ns, PLE_DIM)
    cache_c = cache_ckv.reshape(depth * Bd * past, MLA_KV_LORA)
    cache_r = cache_krope.reshape(depth * Bd * past, MLA_ROPE)

    hp = x_prompt.reshape(Np, D)
    hs = x_sample.reshape(Ns, D)
    lat_p = lat_s = None
    st_p, st_s = [], []
    for i in range(depth):
        final = i == depth - 1

        q, k, v, ckv, kr, rq, rk, rv, rg = _proj(hp, tab_p, norm_mix_w, q_norm_w, kv_norm_w, w, i, lat_p,
                                                 tm=tm_p, n_pos=S // tm_p, n_rep=B, v_rows=True)
        lat_p = (ckv, kr)
        att = _attn_prompt(q.reshape(B, S, -1), k.reshape(B, S, -1),
                           v.reshape(B, S // tm_p, MLA_HEADS, V_ROWS, tm_p), t=tm_p)
        ret, st = _retention(rq.reshape(B, S, -1), rk.reshape(B, S, -1), rv.reshape(B, S, -1), zero_state,
                             L=min(RET_BLOCK, S))
        hp = _post(hp, att.reshape(Np, -1), ret.reshape(Np, -1), rg, pp, ret_gn_w, norm_ffn_w, norm_ple_w,
                   final_norm_w, w, i, tm=tm_p, final=final)
        st_p.append(st)

        q, k, v, ckv, kr, rq, rk, rv, rg = _proj(hs, tab_s, norm_mix_w, q_norm_w, kv_norm_w, w, i, lat_s,
                                                 tm=Ns, n_pos=1, n_rep=1, v_rows=False)
        lat_s = (ckv, kr)
        att = _attn_sample(q.reshape(Bd, Ld, -1), lat_s, cache_c, cache_r, w, i, past=past)
        ret, st = _retention(rq.reshape(Bd, Ld, -1), rk.reshape(Bd, Ld, -1), rv.reshape(Bd, Ld, -1),
                             state_ret[i], L=Ld)
        hs = _post(hs, att.reshape(Ns, -1), ret.reshape(Ns, -1), rg, ps, ret_gn_w, norm_ffn_w, norm_ple_w,
                   final_norm_w, w, i, tm=Ns, final=final)
        st_s.append(st)

    return (hp.reshape(B, S, D), hs.reshape(Bd, Ld, D),
            lat_p[0].reshape(depth, B, S, -1), lat_p[1].reshape(depth, B, S, -1), jnp.stack(st_p),
            lat_s[0].reshape(depth, Bd, Ld, -1), lat_s[1].reshape(depth, Bd, Ld, -1), jnp.stack(st_s))
```
